```python
import math
import jax
import jax.numpy as jnp
from jax import lax
import numpy as np


D_MODEL = 1024
BATCH = 32
SEQ = 2048
DEPTH = 2

CHUNK = 64
D_MIX = D_MODEL
HEAD_DIM = 64
D_RWKV = D_MIX // 2
D_SB = D_MIX - D_RWKV
H_RWKV = D_RWKV // HEAD_DIM
H_SB = D_SB // HEAD_DIM
DECAY_LORA = 64
ICLR_LORA = 64
GATE_LORA = 128
VRES_LORA = 32
C_RWKV = 3 * D_RWKV + DECAY_LORA + ICLR_LORA + GATE_LORA
C_IN = C_RWKV + 3 * D_SB
SB_BLOCK = 128
N_EXPERTS = 32
TOP_K = 4
D_FF_EXPERT = D_MODEL
SWIGLU_ALPHA = 1.702
SWIGLU_LIMIT = 7.0
MOE_BLOCK = 128
NORM_EPS = 1e-6
GN_EPS = 1e-5 * HEAD_DIM
L2_EPS = 1e-12

kernel_name = 'hybrid_rwkv7_stickbreak_moe_adaln'


def rms_norm(x, g):
    xf = x.astype(jnp.float32)
    y = xf * lax.rsqrt(jnp.mean(jnp.square(xf), axis=-1, keepdims=True) + NORM_EPS)
    return (y * g.astype(jnp.float32)).astype(x.dtype)


def token_shift(z, mu):
    z_prev = jnp.pad(z[:, :-1], ((0, 0), (1, 0), (0, 0)))
    return z + (z_prev - z) * mu


def split_heads(t, n_heads):
    bsz, seq, _ = t.shape
    return t.reshape(bsz, seq, n_heads, HEAD_DIM)


def rwkv7_recurrence(r, decay, k, v, kk, a):
    bsz, seq, nh, hd = r.shape
    n_chunks = seq // CHUNK

    def to_chunks(t):
        return jnp.transpose(t, (1, 0, 2, 3)).reshape(n_chunks, CHUNK, bsz, nh, hd)

    def step(state, inp):
        r_t, w_t, k_t, v_t, kk_t, a_t = inp
        s_kk = jnp.einsum('bhvk,bhk->bhv', state, kk_t)
        state = (state * w_t[:, :, None, :]
                 - s_kk[..., None] * (kk_t * a_t)[:, :, None, :]
                 + v_t[..., None] * k_t[:, :, None, :])
        return state, jnp.einsum('bhvk,bhk->bhv', state, r_t)

    def chunk_step(state, chunk):
        return lax.scan(step, state, chunk)

    s0 = jnp.zeros((bsz, nh, hd, hd), jnp.float32)
    xs = tuple(to_chunks(t) for t in (r, decay, k, v, kk, a))
    _, y = lax.scan(chunk_step, s0, xs)
    return jnp.transpose(y.reshape(seq, bsz, nh, hd), (1, 0, 2, 3))


def rwkv7_time_mix(z, v_first, mu, decay_w0, decay_up, iclr_a0, iclr_up, gate_up,
                   k_k, k_a, r_k, lnx_w, lnx_b, vres):
    bsz, seq, _ = z.shape
    zs = token_shift(z, mu)
    r = zs[..., :D_RWKV]
    k = zs[..., D_RWKV:2 * D_RWKV]
    v = zs[..., 2 * D_RWKV:3 * D_RWKV]
    o = 3 * D_RWKV
    w_lo = zs[..., o:o + DECAY_LORA]
    a_lo = zs[..., o + DECAY_LORA:o + DECAY_LORA + ICLR_LORA]
    g_lo = zs[..., o + DECAY_LORA + ICLR_LORA:]
    w_pre = (decay_w0 + jnp.tanh(w_lo) @ decay_up).astype(jnp.float32)
    log_decay = -jnp.exp(-jax.nn.softplus(-w_pre) - 0.5)
    a = jax.nn.sigmoid(iclr_a0 + a_lo @ iclr_up)
    g = jax.nn.sigmoid(g_lo) @ gate_up
    if vres is None:
        v_first = v
    else:
        vres_down, vres_up, vres_b = vres
        v = v + (v_first - v) * jax.nn.sigmoid(vres_b + (v @ vres_down) @ vres_up)
    kk = split_heads((k * k_k).astype(jnp.float32), H_RWKV)
    kk = kk / jnp.maximum(jnp.sqrt(jnp.sum(jnp.square(kk), axis=-1, keepdims=True)), L2_EPS)
    k = k * (1.0 + (a - 1.0) * k_a)
    rh = split_heads(r.astype(jnp.float32), H_RWKV)
    kh = split_heads(k.astype(jnp.float32), H_RWKV)
    vh = split_heads(v.astype(jnp.float32), H_RWKV)
    ah = split_heads(a.astype(jnp.float32), H_RWKV)
    wh = split_heads(jnp.exp(log_decay), H_RWKV)
    y = rwkv7_recurrence(rh, wh, kh, vh, kk, ah)
    mean = jnp.mean(y, axis=-1, keepdims=True)
    var = jnp.mean(jnp.square(y - mean), axis=-1, keepdims=True)
    y = ((y - mean) * lax.rsqrt(var + GN_EPS)).reshape(bsz, seq, D_RWKV)
    y = y * lnx_w.astype(jnp.float32) + lnx_b.astype(jnp.float32)
    bonus = jnp.sum(rh * kh * r_k.astype(jnp.float32), axis=-1, keepdims=True) * vh
    out = (y + bonus.reshape(bsz, seq, D_RWKV)) * g.astype(jnp.float32)
    return out.astype(z.dtype), v_first


def stick_breaking_attention(q, k, v, q_norm_g, k_norm_g, out_g):
    bsz, seq, _ = q.shape
    qh = jnp.transpose(rms_norm(split_heads(q, H_SB), q_norm_g).astype(jnp.float32), (0, 2, 1, 3))
    kh = jnp.transpose(rms_norm(split_heads(k, H_SB), k_norm_g).astype(jnp.float32), (0, 2, 1, 3))
    vh = jnp.transpose(split_heads(v, H_SB).astype(jnp.float32), (0, 2, 1, 3))
    scale = 1.0 / math.sqrt(HEAD_DIM)
    blocks = []
    for blk in range(seq // SB_BLOCK):
        t0 = blk * SB_BLOCK
        t1 = t0 + SB_BLOCK
        logits = jnp.einsum('bhqd,bhkd->bhqk', qh[:, :, t0:t1], kh[:, :, :t1]) * scale
        mask = jnp.arange(t1)[None, :] < jnp.arange(t0, t1)[:, None]
        log_keep = jnp.where(mask, jax.nn.log_sigmoid(-logits), 0.0)
        log_between = lax.cumsum(log_keep, axis=3, reverse=True) - log_keep
        weight = jnp.where(mask, jnp.exp(jax.nn.log_sigmoid(logits) + log_between), 0.0)
        blocks.append(jnp.einsum('bhqk,bhkd->bhqd', weight, vh[:, :, :t1]))
    o = jnp.transpose(jnp.concatenate(blocks, axis=2), (0, 2, 1, 3))
    o = rms_norm(o, out_g.reshape(H_SB, HEAD_DIM))
    return o.reshape(bsz, seq, D_SB).astype(q.dtype)


def moe_ffn(h, router_w, router_b, w1, b1, w2, b2):
    bsz, seq, d = h.shape
    n_tok = bsz * seq
    xt = h.reshape(n_tok, d)
    logits = xt.astype(jnp.float32) @ router_w.astype(jnp.float32) + router_b.astype(jnp.float32)
    top_logit, top_idx = lax.top_k(logits, TOP_K)
    top_gate = jax.nn.softmax(top_logit, axis=-1)
    n_assign = n_tok * TOP_K
    flat_e = top_idx.reshape(-1)
    flat_tok = jnp.arange(n_assign, dtype=jnp.int32) // TOP_K
    order = jnp.argsort(flat_e, stable=True)
    e_sorted = flat_e[order]
    counts = jnp.bincount(flat_e, length=N_EXPERTS)
    padded = (counts + MOE_BLOCK - 1) // MOE_BLOCK * MOE_BLOCK
    padded_end = jnp.cumsum(padded)
    padded_start = padded_end - padded
    start = jnp.cumsum(counts) - counts
    dest = padded_start[e_sorted] + jnp.arange(n_assign, dtype=jnp.int32) - start[e_sorted]
    n_rows = n_assign + N_EXPERTS * MOE_BLOCK
    n_blocks = n_rows // MOE_BLOCK
    row_tok = jnp.zeros((n_rows,), jnp.int32).at[dest].set(flat_tok[order])
    row_gate = jnp.zeros((n_rows,), h.dtype).at[dest].set(top_gate.reshape(-1)[order].astype(h.dtype))
    block_expert = jnp.minimum(
        jnp.searchsorted(padded_end, jnp.arange(n_blocks, dtype=jnp.int32) * MOE_BLOCK, side='right'),
        N_EXPERTS - 1)

    def expert_block(args):
        tok, gate, e = args
        xb = xt[tok]
        hid = xb @ w1[e] + b1[e]
        glu = jnp.minimum(hid[:, :D_FF_EXPERT], SWIGLU_LIMIT)
        lin = jnp.clip(hid[:, D_FF_EXPERT:], -SWIGLU_LIMIT, SWIGLU_LIMIT)
        act = glu * jax.nn.sigmoid(SWIGLU_ALPHA * glu) * (lin + 1.0)
        return (act @ w2[e] + b2[e]) * gate[:, None]

    y = lax.map(expert_block, (row_tok.reshape(n_blocks, MOE_BLOCK),
                               row_gate.reshape(n_blocks, MOE_BLOCK), block_expert))
    out = jax.ops.segment_sum(y.reshape(n_rows, d), row_tok, num_segments=n_tok)
    return out.reshape(bsz, seq, d)


def setup_inputs(seed: int = 0) -> dict:
    key = jax.random.key(seed)
    ks = iter(jax.random.split(key, 40))

    def normal(shape, scale):
        return jax.random.normal(next(ks), shape, jnp.float32) * scale

    def gain(shape, s=0.02):
        return 1.0 + normal(shape, s)

    L = DEPTH
    LV = max(DEPTH - 1, 0)
    return {
        'x': normal((BATCH, SEQ, D_MODEL), 1.0),
        'c': normal((BATCH, D_MODEL), 1.0),
        'norm1_g': gain((L, D_MODEL)),
        'norm2_g': gain((L, D_MODEL)),
        'ada_w': normal((L, D_MODEL, 6 * D_MODEL), 0.5 * D_MODEL ** -0.5),
        'ada_b': normal((L, 6 * D_MODEL), 0.02),
        'w_in': normal((L, D_MODEL, C_IN), D_MODEL ** -0.5),
        'shift_mu': jax.random.uniform(next(ks), (L, C_RWKV), jnp.float32),
        'decay_w0': jax.random.uniform(next(ks), (L, D_RWKV), jnp.float32, -1.5, 1.5),
        'decay_up': normal((L, DECAY_LORA, D_RWKV), 0.5 * DECAY_LORA ** -0.5),
        'iclr_a0': normal((L, D_RWKV), 0.1),
        'iclr_up': normal((L, ICLR_LORA, D_RWKV), 0.5 * ICLR_LORA ** -0.5),
        'gate_up': normal((L, GATE_LORA, D_RWKV), GATE_LORA ** -0.5),
        'k_k': gain((L, D_RWKV), 0.1),
        'k_a': gain((L, D_RWKV), 0.1),
        'r_k': normal((L, H_RWKV, HEAD_DIM), 0.1),
        'lnx_w': gain((L, D_RWKV)),
        'lnx_b': normal((L, D_RWKV), 0.02),
        'vres_down': normal((LV, D_RWKV, VRES_LORA), D_RWKV ** -0.5),
        'vres_up': normal((LV, VRES_LORA, D_RWKV), 0.5 * VRES_LORA ** -0.5),
        'vres_b': normal((LV, D_RWKV), 0.1),
        'q_norm_g': gain((L, HEAD_DIM)),
        'k_norm_g': gain((L, HEAD_DIM)),
        'sb_out_g': gain((L, D_SB)),
        'w_out': normal((L, D_MIX, D_MODEL), D_MIX ** -0.5),
        'router_w': normal((L, D_MODEL, N_EXPERTS), D_MODEL ** -0.5),
        'router_b': normal((L, N_EXPERTS), 0.01),
        'exp_w1': normal((L, N_EXPERTS, D_MODEL, 2 * D_FF_EXPERT), D_MODEL ** -0.5),
        'exp_b1': normal((L, N_EXPERTS, 2 * D_FF_EXPERT), 0.02),
        'exp_w2': normal((L, N_EXPERTS, D_FF_EXPERT, D_MODEL), D_FF_EXPERT ** -0.5),
        'exp_b2': normal((L, N_EXPERTS, D_MODEL), 0.02),
    }


def reference(x, c, norm1_g, norm2_g, ada_w, ada_b, w_in, shift_mu, decay_w0, decay_up,
              iclr_a0, iclr_up, gate_up, k_k, k_a, r_k, lnx_w, lnx_b, vres_down, vres_up,
              vres_b, q_norm_g, k_norm_g, sb_out_g, w_out, router_w, router_b,
              exp_w1, exp_b1, exp_w2, exp_b2):
    cond = jax.nn.silu(c)
    v_first = None
    for l in range(DEPTH):
        mod = cond @ ada_w[l] + ada_b[l]
        shift1, scale1, gate1, shift2, scale2, gate2 = jnp.split(mod[:, None, :], 6, axis=-1)
        h = rms_norm(x, norm1_g[l]) * (1.0 + scale1) + shift1
        z = h @ w_in[l]
        vres = None if l == 0 else (vres_down[l - 1], vres_up[l - 1], vres_b[l - 1])
        y_rwkv, v_first = rwkv7_time_mix(
            z[..., :C_RWKV], v_first, shift_mu[l], decay_w0[l], decay_up[l], iclr_a0[l],
            iclr_up[l], gate_up[l], k_k[l], k_a[l], r_k[l], lnx_w[l], lnx_b[l], vres)
        z_sb = z[..., C_RWKV:]
        y_sb = stick_breaking_attention(
            z_sb[..., :D_SB], z_sb[..., D_SB:2 * D_SB], z_sb[..., 2 * D_SB:],
            q_norm_g[l], k_norm_g[l], sb_out_g[l])
        mixed = jnp.concatenate([y_rwkv, y_sb], axis=-1) @ w_out[l]
        x = x + gate1 * mixed
        h2 = rms_norm(x, norm2_g[l]) * (1.0 + scale2) + shift2
        x = x + gate2 * moe_ffn(h2, router_w[l], router_b[l], exp_w1[l], exp_b1[l],
                                exp_w2[l], exp_b2[l])
    return x
```

```python
import functools
import math

import jax
import jax.numpy as jnp
from jax import lax
from jax.experimental import pallas as pl
from jax.experimental.pallas import tpu as pltpu

F32 = jnp.float32
BF = jnp.bfloat16

HEAD_DIM = 64
LANES = 128
RWKV_CHUNK = 64
DECAY_LORA = 64
ICLR_LORA = 64
GATE_LORA = 128
N_EXPERTS = 32
TOP_K = 4
SWIGLU_ALPHA = 1.702
SWIGLU_LIMIT = 7.0
NORM_EPS = 1e-6
GN_EPS = 1e-5 * HEAD_DIM
L2_EPS = 1e-12

ROW_TILE = 256
SB_TILE = 128
MOE_TILE = 512
VMEM_LIMIT = 48 * 1024 * 1024


def _dot(a, b):
    return jnp.dot(a, b, preferred_element_type=F32)


def _dot_nt(a, b):
    return lax.dot_general(a, b, (((1,), (1,)), ((), ())), preferred_element_type=F32)


def _split2(x):
    hi = x.astype(BF)
    lo = (x - hi.astype(F32)).astype(BF)
    return hi, lo


def _dot_lhs2(x, m_bf):
    hi, lo = _split2(x)
    return _dot(hi, m_bf) + _dot(lo, m_bf)


def _dot3(a, b):
    ah, al = _split2(a)
    bh, bl = _split2(b)
    return _dot(ah, bh) + (_dot(ah, bl) + _dot(al, bh))


def _params(*sem):
    return pltpu.CompilerParams(dimension_semantics=sem, vmem_limit_bytes=VMEM_LIMIT)


def _ada_kernel(c_ref, w_ref, b_ref, o_ref):
    c = c_ref[...]
    cond = c * jax.nn.sigmoid(c)
    o_ref[0] = _dot3(cond, w_ref[0]) + b_ref[0]


def _ada_mod(c, ada_w, ada_b):
    depth, d, d6 = ada_w.shape
    bsz = c.shape[0]
    out = pl.pallas_call(
        _ada_kernel,
        grid=(depth, d6 // d),
        in_specs=[
            pl.BlockSpec((bsz, d), lambda l, j: (0, 0)),
            pl.BlockSpec((1, d, d), lambda l, j: (l, 0, j)),
            pl.BlockSpec((1, 1, d), lambda l, j: (l, 0, j)),
        ],
        out_specs=pl.BlockSpec((1, bsz, d), lambda l, j: (l, 0, j)),
        out_shape=jax.ShapeDtypeStruct((depth, bsz, d6), F32),
        compiler_params=_params("arbitrary", "arbitrary"),
        name="ada_mod",
    )(c, ada_w, ada_b.reshape(depth, 1, d6))
    return out.reshape(depth, bsz, d6 // d, d)


def _in_proj_kernel(c_rwkv, d_sb, x_ref, mod_ref, g_ref, w_ref, bdm_ref, qg_ref, kg_ref,
                    zr_ref, q_ref, k_ref, v_ref):
    x = x_ref[...]
    ms = jnp.mean(x * x, axis=-1, keepdims=True)
    h = x * lax.rsqrt(ms + NORM_EPS) * g_ref[...]
    h = h * (1.0 + mod_ref[0, 1:2, :]) + mod_ref[0, 0:1, :]
    z = _dot(h.astype(BF), w_ref[...])
    zr_ref[...] = z[:, :c_rwkv]
    bdm = bdm_ref[...]
    scale = 1.0 / math.sqrt(HEAD_DIM)
    for p in range(d_sb // LANES):
        lo = p * LANES
        q = z[:, c_rwkv + lo:c_rwkv + lo + LANES]
        k = z[:, c_rwkv + d_sb + lo:c_rwkv + d_sb + lo + LANES]
        qn = q * lax.rsqrt(_dot_lhs2(q * q, bdm) + NORM_EPS) * qg_ref[...]
        kn = k * lax.rsqrt(_dot_lhs2(k * k, bdm) + NORM_EPS) * kg_ref[...]
        q_ref[:, lo:lo + LANES] = (qn * scale).astype(BF)
        k_ref[:, lo:lo + LANES] = kn.astype(BF)
    v_ref[...] = z[:, c_rwkv + 2 * d_sb:].astype(BF)


def _in_proj(x2, mod, g, w_bf, bdm, qg, kg, seq, c_rwkv, d_sb):
    n, d = x2.shape
    c_in = w_bf.shape[1]
    tm = ROW_TILE
    per_seq = seq // tm
    row = lambda i: (i, 0)
    const = lambda i: (0, 0)
    return pl.pallas_call(
        functools.partial(_in_proj_kernel, c_rwkv, d_sb),
        grid=(n // tm,),
        in_specs=[
            pl.BlockSpec((tm, d), row),
            pl.BlockSpec((1,) + mod.shape[1:], lambda i: (i // per_seq, 0, 0)),
            pl.BlockSpec((1, d), const),
            pl.BlockSpec((d, c_in), const),
            pl.BlockSpec((LANES, LANES), const),
            pl.BlockSpec((1, LANES), const),
            pl.BlockSpec((1, LANES), const),
        ],
        out_specs=[
            pl.BlockSpec((tm, c_rwkv), row),
            pl.BlockSpec((tm, d_sb), row),
            pl.BlockSpec((tm, d_sb), row),
            pl.BlockSpec((tm, d_sb), row),
        ],
        out_shape=[
            jax.ShapeDtypeStruct((n, c_rwkv), F32),
            jax.ShapeDtypeStruct((n, d_sb), BF),
            jax.ShapeDtypeStruct((n, d_sb), BF),
            jax.ShapeDtypeStruct((n, d_sb), BF),
        ],
        compiler_params=_params("arbitrary"),
        name="in_proj",
    )(x2, mod, g, w_bf, bdm, qg, kg)


def _rwkv_kernel(r_ref, lw_ref, k_ref, v_ref, kk_ref, a_ref, g_ref, rk_ref, lnw_ref, lnb_ref,
                 tri_ref, bdm_ref, o_ref, st_ref):
    chunk = RWKV_CHUNK
    two = 2 * chunk

    @pl.when(pl.program_id(1) == 0)
    def _():
        st_ref[...] = jnp.zeros_like(st_ref)

    row = lax.broadcasted_iota(jnp.int32, (two, LANES), 0)
    col = lax.broadcasted_iota(jnp.int32, (two, LANES), 1)
    same_head = (row < chunk) == (col < HEAD_DIM)
    row_in = row & (chunk - 1)
    col_in = col & (chunk - 1)
    strict = row_in > col_in
    incl = row_in >= col_in
    eye = (row == col).astype(F32)
    tri = tri_ref[...]
    bdm = bdm_ref[...]

    def bd(t):
        return jnp.where(same_head, jnp.concatenate([t, t], axis=0), 0.0)

    for p in range(r_ref.shape[2] // LANES):
        sl = slice(p * LANES, (p + 1) * LANES)
        r = r_ref[0, :, sl]
        lw = lw_ref[0, :, sl]
        k = k_ref[0, :, sl]
        v = v_ref[0, :, sl]
        kk = kk_ref[0, :, sl]
        a = a_ref[0, :, sl]

        l1 = lw.astype(BF)
        rem = lw - l1.astype(F32)
        l2 = rem.astype(BF)
        l3 = (rem - l2.astype(F32)).astype(BF)
        cum = _dot(tri, l1) + (_dot(tri, l2) + _dot(tri, l3))
        total = cum[chunk - 1:chunk, :]
        dec_in = jnp.exp(cum)
        dec_ex = jnp.exp(cum - lw)
        grow = jnp.exp(-cum)
        to_end = jnp.exp(total - cum)

        b = kk * a
        a_s = bd(-(kk * dec_ex)).astype(BF)
        r_s = bd(r * dec_in).astype(BF)
        b_s = bd(b * grow).astype(BF)
        k_s = bd(k * grow).astype(BF)
        v_bd = bd(v).astype(BF)

        pair = _dot_nt(jnp.concatenate([a_s, r_s], axis=0), jnp.concatenate([b_s, k_s], axis=0))
        n_ab = jnp.where(strict, pair[:two, :two], 0.0)
        a_ak = jnp.where(strict, pair[:two, two:], 0.0)
        a_rb = jnp.where(incl, pair[two:, :two], 0.0)
        a_rk = jnp.where(incl, pair[two:, two:], 0.0)

        inv = eye + n_ab
        power = n_ab
        for _ in range(int(math.log2(chunk)) - 1):
            pb = power.astype(BF)
            power = _dot(pb, pb)
            inv = inv + _dot(inv.astype(BF), power.astype(BF))

        st = st_ref[p]
        st_b = st.astype(BF)
        x0 = _dot(jnp.concatenate([a_s, a_ak.astype(BF)], axis=1),
                  jnp.concatenate([st_b, v_bd], axis=0))
        u = _dot(inv.astype(BF), x0.astype(BF))
        u_b = u.astype(BF)
        y_bd = _dot(jnp.concatenate([r_s, a_rb.astype(BF), a_rk.astype(BF)], axis=1),
                    jnp.concatenate([st_b, u_b, v_bd], axis=0))

        bk_t = jnp.concatenate([bd(b * to_end), bd(k * to_end)], axis=0).T.astype(BF)
        decay_col = jnp.broadcast_to(jnp.exp(total), (two, LANES)).T
        st_ref[p] = decay_col * st + _dot(bk_t, jnp.concatenate([u_b, v_bd], axis=0))

        y = y_bd[:chunk] + y_bd[chunk:]
        mean = _dot_lhs2(y, bdm)
        cen = y - mean
        var = _dot_lhs2(cen * cen, bdm)
        yn = cen * lax.rsqrt(var + GN_EPS) * lnw_ref[:, sl] + lnb_ref[:, sl]
        bonus = _dot_lhs2(r * k * rk_ref[:, sl], bdm) * float(HEAD_DIM) * v
        o_ref[0, :, sl] = ((yn + bonus) * g_ref[0, :, sl]).astype(o_ref.dtype)


def _rwkv_mix(r, lw, k, v, kk, a, g, rk, lnw, lnb, tri, bdm):
    bsz, seq, dr = r.shape
    chunk = RWKV_CHUNK
    tok = pl.BlockSpec((1, chunk, dr), lambda b, c: (b, c, 0))
    vec = pl.BlockSpec((1, dr), lambda b, c: (0, 0))
    return pl.pallas_call(
        _rwkv_kernel,
        grid=(bsz, seq // chunk),
        in_specs=[tok] * 7 + [vec] * 3 + [
            pl.BlockSpec((chunk, chunk), lambda b, c: (0, 0)),
            pl.BlockSpec((LANES, LANES), lambda b, c: (0, 0)),
        ],
        out_specs=tok,
        out_shape=jax.ShapeDtypeStruct((bsz, seq, dr), BF),
        scratch_shapes=[pltpu.VMEM((dr // LANES, LANES, LANES), F32)],
        compiler_params=_params("arbitrary", "arbitrary"),
        name="rwkv_mix",
    )(r, lw, k, v, kk, a, g, rk, lnw, lnb, tri, bdm)


def _rwkv_prep(z, v_first, mu, decay_w0, decay_up, iclr_a0, iclr_up, gate_up, k_k, k_a, vres):
    d_r = decay_w0.shape[0]
    z_prev = jnp.pad(z[:, :-1], ((0, 0), (1, 0), (0, 0)))
    zs = z + (z_prev - z) * mu
    r = zs[..., :d_r]
    k = zs[..., d_r:2 * d_r]
    v = zs[..., 2 * d_r:3 * d_r]
    o = 3 * d_r
    w_lo = zs[..., o:o + DECAY_LORA]
    a_lo = zs[..., o + DECAY_LORA:o + DECAY_LORA + ICLR_LORA]
    g_lo = zs[..., o + DECAY_LORA + ICLR_LORA:]
    w_pre = decay_w0 + jnp.tanh(w_lo) @ decay_up
    log_decay = -jnp.exp(-jax.nn.softplus(-w_pre) - 0.5)
    a = jax.nn.sigmoid(iclr_a0 + a_lo @ iclr_up)
    g = jax.nn.sigmoid(g_lo) @ gate_up
    if vres is None:
        v_first = v
    else:
        vres_down, vres_up, vres_b = vres
        v = v + (v_first - v) * jax.nn.sigmoid(vres_b + (v @ vres_down) @ vres_up)
    kk = k * k_k
    kh = kk.reshape(kk.shape[:-1] + (d_r // HEAD_DIM, HEAD_DIM))
    kh = kh / jnp.maximum(jnp.sqrt(jnp.sum(jnp.square(kh), axis=-1, keepdims=True)), L2_EPS)
    kk = kh.reshape(kk.shape)
    k = k * (1.0 + (a - 1.0) * k_a)
    return (r, log_decay, k, v, kk, a, g), v_first


def _sb_kernel(q_ref, k_ref, v_ref, og_ref, tri_ref, bdm_ref, o_ref, acc_ref, carry_ref):
    t = SB_TILE
    qi = pl.program_id(2)
    q = q_ref[0]
    lane = lax.broadcasted_iota(jnp.int32, (t, LANES), 1)
    rowi = lax.broadcasted_iota(jnp.int32, (t, t), 0)
    coli = lax.broadcasted_iota(jnp.int32, (t, t), 1)
    causal = coli < rowi
    zero = jnp.zeros_like(q)
    q_heads = [jnp.where(lane < HEAD_DIM, q, zero), jnp.where(lane >= HEAD_DIM, q, zero)]
    tri = tri_ref[...]

    acc_ref[...] = jnp.zeros_like(acc_ref)
    carry_ref[...] = jnp.zeros_like(carry_ref)

    def block(j, masked):
        start = pl.multiple_of(j * t, t)
        kb = k_ref[0, pl.ds(start, t), :]
        vb = v_ref[0, pl.ds(start, t), :]
        for h in range(2):
            s = _dot_nt(q_heads[h], kb)
            soft = jnp.maximum(s, 0.0) + jnp.log1p(jnp.exp(-jnp.abs(s)))
            log_keep = -soft
            if masked:
                log_keep = jnp.where(causal, log_keep, 0.0)
            sums = _dot_lhs2(log_keep, tri)
            w = jnp.exp((s - soft) + sums[:, :t] + carry_ref[h])
            if masked:
                w = jnp.where(causal, w, 0.0)
            acc_ref[h] += _dot(w.astype(BF), vb)
            carry_ref[h] += sums[:, t:]

    block(qi, True)

    def body(i, c):
        block(qi - 1 - i, False)
        return c

    lax.fori_loop(0, qi, body, 0)

    o = jnp.where(lane < HEAD_DIM, acc_ref[0], acc_ref[1])
    ms = _dot_lhs2(o * o, bdm_ref[...])
    o_ref[0] = (o * lax.rsqrt(ms + NORM_EPS) * og_ref[...]).astype(o_ref.dtype)


def _sb_attention(q, k, v, og, tri2, bdm):
    bsz, seq, ds = q.shape
    t = SB_TILE
    return pl.pallas_call(
        _sb_kernel,
        grid=(bsz, ds // LANES, seq // t),
        in_specs=[
            pl.BlockSpec((1, t, LANES), lambda b, p, i: (b, i, p)),
            pl.BlockSpec((1, seq, LANES), lambda b, p, i: (b, 0, p)),
            pl.BlockSpec((1, seq, LANES), lambda b, p, i: (b, 0, p)),
            pl.BlockSpec((1, LANES), lambda b, p, i: (0, p)),
            pl.BlockSpec((t, 2 * t), lambda b, p, i: (0, 0)),
            pl.BlockSpec((LANES, LANES), lambda b, p, i: (0, 0)),
        ],
        out_specs=pl.BlockSpec((1, t, LANES), lambda b, p, i: (b, i, p)),
        out_shape=jax.ShapeDtypeStruct((bsz, seq, ds), BF),
        scratch_shapes=[pltpu.VMEM((2, t, LANES), F32), pltpu.VMEM((2, t, t), F32)],
        compiler_params=_params("arbitrary", "arbitrary", "arbitrary"),
        name="sb_attention",
    )(q, k, v, og, tri2, bdm)


def _out_proj_kernel(yr_ref, ys_ref, x_ref, mod_ref, g_ref, wr_ref, ws_ref, rwh_ref, rwl_ref, rb_ref,
                     xo_ref, h_ref, idx_ref, gate_ref):
    mixed = _dot(yr_ref[...], wr_ref[...]) + _dot(ys_ref[...], ws_ref[...])
    x = x_ref[...] + mod_ref[0, 2:3, :] * mixed
    xo_ref[...] = x
    ms = jnp.mean(x * x, axis=-1, keepdims=True)
    h = x * lax.rsqrt(ms + NORM_EPS) * g_ref[...]
    h = h * (1.0 + mod_ref[0, 4:5, :]) + mod_ref[0, 3:4, :]
    h_ref[...] = h.astype(BF)

    hh, hl = _split2(h)
    rwh = rwh_ref[...]
    logits = _dot_nt(rwh, hh) + (_dot_nt(rwh, hl) + _dot_nt(rwl_ref[...], hh)) + rb_ref[...]
    eidx = lax.broadcasted_iota(jnp.int32, logits.shape, 0)
    tops, idxs = [], []
    for _ in range(TOP_K):
        m = jnp.max(logits, axis=0, keepdims=True)
        idx = jnp.min(jnp.where(logits == m, eidx, N_EXPERTS), axis=0, keepdims=True)
        tops.append(m)
        idxs.append(idx)
        logits = jnp.where(eidx == idx, -jnp.inf, logits)
    exps = [jnp.exp(m - tops[0]) for m in tops]
    den = exps[0] + exps[1] + exps[2] + exps[3]
    pad = [jnp.zeros_like(den)] * (8 - TOP_K)
    gate_ref[...] = jnp.concatenate([e / den for e in exps] + pad, axis=0)
    idx_ref[...] = jnp.concatenate(idxs + [jnp.zeros_like(idxs[0])] * (8 - TOP_K), axis=0)


def _out_proj(yr, ys, x2, mod, g, wr, ws, rwh, rwl, rb, seq):
    n, d = x2.shape
    dr = yr.shape[1]
    ds = ys.shape[1]
    tm = ROW_TILE
    per_seq = seq // tm
    row = lambda i: (i, 0)
    const = lambda i: (0, 0)
    colblk = lambda i: (0, i)
    return pl.pallas_call(
        _out_proj_kernel,
        grid=(n // tm,),
        in_specs=[
            pl.BlockSpec((tm, dr), row),
            pl.BlockSpec((tm, ds), row),
            pl.BlockSpec((tm, d), row),
            pl.BlockSpec((1,) + mod.shape[1:], lambda i: (i // per_seq, 0, 0)),
            pl.BlockSpec((1, d), const),
            pl.BlockSpec((dr, d), const),
            pl.BlockSpec((ds, d), const),
            pl.BlockSpec((N_EXPERTS, d), const),
            pl.BlockSpec((N_EXPERTS, d), const),
            pl.BlockSpec((N_EXPERTS, 1), const),
        ],
        out_specs=[
            pl.BlockSpec((tm, d), row),
            pl.BlockSpec((tm, d), row),
            pl.BlockSpec((8, tm), colblk),
            pl.BlockSpec((8, tm), colblk),
        ],
        out_shape=[
            jax.ShapeDtypeStruct((n, d), F32),
            jax.ShapeDtypeStruct((n, d), BF),
            jax.ShapeDtypeStruct((8, n), jnp.int32),
            jax.ShapeDtypeStruct((8, n), F32),
        ],
        compiler_params=_params("arbitrary"),
        name="out_proj_router",
    )(yr, ys, x2, mod, g, wr, ws, rwh, rwl, rb)


def _moe_kernel(te_ref, nu_ref, x_ref, w1_ref, b1_ref, w2_ref, b2_ref, o_ref):
    i = pl.program_id(0)
    f = w2_ref.shape[1]

    @pl.when(i < nu_ref[0])
    def _():
        hid = _dot(x_ref[...], w1_ref[0]) + b1_ref[0]
        glu = jnp.minimum(hid[:, :f], SWIGLU_LIMIT)
        lin = jnp.clip(hid[:, f:], -SWIGLU_LIMIT, SWIGLU_LIMIT)
        act = glu * jax.nn.sigmoid(SWIGLU_ALPHA * glu) * (lin + 1.0)
        o_ref[...] = (_dot(act.astype(BF), w2_ref[0]) + b2_ref[0]).astype(o_ref.dtype)

    @pl.when(i >= nu_ref[0])
    def _():
        o_ref[...] = jnp.zeros_like(o_ref)


def _moe_experts(tile_expert, n_used, xs, w1, b1, w2, b2):
    n_rows, d = xs.shape
    f2 = w1.shape[2]
    f = w2.shape[1]
    tm = MOE_TILE
    return pl.pallas_call(
        _moe_kernel,
        grid_spec=pltpu.PrefetchScalarGridSpec(
            num_scalar_prefetch=2,
            grid=(n_rows // tm,),
            in_specs=[
                pl.BlockSpec((tm, d), lambda i, te, nu: (i, 0)),
                pl.BlockSpec((1, d, f2), lambda i, te, nu: (te[i], 0, 0)),
                pl.BlockSpec((1, 1, f2), lambda i, te, nu: (te[i], 0, 0)),
                pl.BlockSpec((1, f, d), lambda i, te, nu: (te[i], 0, 0)),
                pl.BlockSpec((1, 1, d), lambda i, te, nu: (te[i], 0, 0)),
            ],
            out_specs=pl.BlockSpec((tm, d), lambda i, te, nu: (i, 0)),
        ),
        out_shape=jax.ShapeDtypeStruct((n_rows, d), F32),
        compiler_params=_params("arbitrary"),
        name="moe_experts",
    )(tile_expert, n_used, xs, w1, b1.reshape(b1.shape[0], 1, f2), w2, b2.reshape(b2.shape[0], 1, d))


def _moe_ffn(h_bf, top_idx, top_gate, w1, b1, w2, b2):
    n_tok, d = h_bf.shape
    tm = MOE_TILE
    n_assign = n_tok * TOP_K
    flat_e = top_idx.reshape(-1)
    flat_tok = jnp.tile(jnp.arange(n_tok, dtype=jnp.int32), TOP_K)
    order = jnp.argsort(flat_e, stable=True)
    e_sorted = flat_e[order]
    counts = jnp.bincount(flat_e, length=N_EXPERTS).astype(jnp.int32)
    padded = (counts + tm - 1) // tm * tm
    padded_end = jnp.cumsum(padded)
    padded_start = padded_end - padded
    start = jnp.cumsum(counts) - counts
    dest = padded_start[e_sorted] + jnp.arange(n_assign, dtype=jnp.int32) - start[e_sorted]
    n_rows = n_assign + N_EXPERTS * tm
    n_tiles = n_rows // tm
    row_tok = jnp.zeros((n_rows,), jnp.int32).at[dest].set(flat_tok[order])
    pos = jnp.zeros((n_assign,), jnp.int32).at[order].set(dest).reshape(TOP_K, n_tok)
    tile_expert = jnp.minimum(
        jnp.searchsorted(padded_end, jnp.arange(n_tiles, dtype=jnp.int32) * tm, side='right'),
        N_EXPERTS - 1).astype(jnp.int32)
    n_used = (padded_end[-1:] // tm).astype(jnp.int32)
    xs = h_bf[row_tok]
    ys = _moe_experts(tile_expert, n_used, xs, w1, b1, w2, b2)
    out = ys[pos[0]] * top_gate[0][:, None]
    for s in range(1, TOP_K):
        out = out + ys[pos[s]] * top_gate[s][:, None]
    return out


def _block_diag_mean():
    i = jnp.arange(LANES)
    return jnp.where((i[:, None] // HEAD_DIM) == (i[None, :] // HEAD_DIM), 1.0 / HEAD_DIM, 0.0).astype(BF)


@jax.jit
def _forward(x, c, norm1_g, norm2_g, ada_w, ada_b, w_in, shift_mu, decay_w0, decay_up, iclr_a0,
             iclr_up, gate_up, k_k, k_a, r_k, lnx_w, lnx_b, vres_down, vres_up, vres_b, q_norm_g,
             k_norm_g, sb_out_g, w_out, router_w, router_b, exp_w1, exp_b1, exp_w2, exp_b2):
    bsz, seq, d = x.shape
    depth = w_in.shape[0]
    d_r = decay_w0.shape[1]
    c_rwkv = shift_mu.shape[1]
    d_sb = sb_out_g.shape[1]
    n_tok = bsz * seq
    assert seq % ROW_TILE == 0 and seq % SB_TILE == 0 and seq % RWKV_CHUNK == 0

    bdm = _block_diag_mean()
    ci = jnp.arange(RWKV_CHUNK)
    tri_chunk = (ci[None, :] <= ci[:, None]).astype(BF)
    si = jnp.arange(SB_TILE)
    tri_sb = jnp.concatenate([(si[:, None] > si[None, :]).astype(BF),
                              jnp.ones((SB_TILE, SB_TILE), BF)], axis=1)

    mod_all = _ada_mod(c, ada_w, ada_b)
    x2 = x.reshape(n_tok, d)
    v_first = None
    for l in range(depth):
        mod = mod_all[l]
        zr, q, k, v = _in_proj(
            x2, mod, norm1_g[l].reshape(1, d), w_in[l].astype(BF), bdm,
            jnp.tile(q_norm_g[l], LANES // HEAD_DIM).reshape(1, LANES),
            jnp.tile(k_norm_g[l], LANES // HEAD_DIM).reshape(1, LANES), seq, c_rwkv, d_sb)
        vres = None if l == 0 else (vres_down[l - 1], vres_up[l - 1], vres_b[l - 1])
        ops, v_first = _rwkv_prep(zr.reshape(bsz, seq, c_rwkv), v_first, shift_mu[l], decay_w0[l],
                                  decay_up[l], iclr_a0[l], iclr_up[l], gate_up[l], k_k[l], k_a[l], vres)
        y_r = _rwkv_mix(*ops, r_k[l].reshape(1, d_r), lnx_w[l].reshape(1, d_r),
                        lnx_b[l].reshape(1, d_r), tri_chunk, bdm)
        y_s = _sb_attention(q.reshape(bsz, seq, d_sb), k.reshape(bsz, seq, d_sb),
                            v.reshape(bsz, seq, d_sb), sb_out_g[l].reshape(1, d_sb), tri_sb, bdm)
        rw_t = router_w[l].T
        rwh = rw_t.astype(BF)
        rwl = (rw_t - rwh.astype(F32)).astype(BF)
        w_o = w_out[l].astype(BF)
        x_mid, h2, top_idx, top_gate = _out_proj(
            y_r.reshape(n_tok, d_r), y_s.reshape(n_tok, d_sb), x2, mod, norm2_g[l].reshape(1, d),
            w_o[:d_r], w_o[d_r:], rwh, rwl, router_b[l].reshape(N_EXPERTS, 1), seq)
        moe = _moe_ffn(h2, top_idx[:TOP_K], top_gate[:TOP_K], exp_w1[l].astype(BF), exp_b1[l],
                       exp_w2[l].astype(BF), exp_b2[l])
        gate2 = jnp.repeat(mod[:, 5, :], seq, axis=0)
        x2 = x_mid + gate2 * moe
    return x2.reshape(bsz, seq, d)


def kernel(x, c, norm1_g, norm2_g, ada_w, ada_b, w_in, shift_mu, decay_w0, decay_up, iclr_a0, iclr_up,
           gate_up, k_k, k_a, r_k, lnx_w, lnx_b, vres_down, vres_up, vres_b, q_norm_g, k_norm_g,
           sb_out_g, w_out, router_w, router_b, exp_w1, exp_b1, exp_w2, exp_b2):
    return _forward(x, c, norm1_g, norm2_g, ada_w, ada_b, w_in, shift_mu, decay_w0, decay_up, iclr_a0,
                    iclr_up, gate_up, k_k, k_a, r_k, lnx_w, lnx_b, vres_down, vres_up, vres_b,
                    q_norm_g, k_norm_g, sb_out_g, w_out, router_w, router_b, exp_w1, exp_b1,
                    exp_w2, exp_b2)
```

```python
import functools
import math

import jax
import jax.numpy as jnp
from jax import lax
from jax.experimental import pallas as pl
from jax.experimental.pallas import tpu as pltpu

F32 = jnp.float32
BF = jnp.bfloat16

HEAD_DIM = 64
LANES = 128
RWKV_CHUNK = 64
RWKV_SEQS = 2
DECAY_LORA = 64
ICLR_LORA = 64
GATE_LORA = 128
N_EXPERTS = 32
TOP_K = 4
SWIGLU_ALPHA = 1.702
SWIGLU_LIMIT = 7.0
NORM_EPS = 1e-6
GN_EPS = 1e-5 * HEAD_DIM
L2_EPS = 1e-12

ROW_TILE = 256
SB_TILE = 256
MOE_TILE = 512
VMEM_LIMIT = 48 * 1024 * 1024


def _dot(a, b):
    return jnp.dot(a, b, preferred_element_type=F32)


def _dot_nt(a, b):
    return lax.dot_general(a, b, (((1,), (1,)), ((), ())), preferred_element_type=F32)


def _split2(x):
    hi = x.astype(BF)
    lo = (x - hi.astype(F32)).astype(BF)
    return hi, lo


def _dot_lhs2(x, m_bf):
    hi, lo = _split2(x)
    return _dot(hi, m_bf) + _dot(lo, m_bf)


def _dot3(a, b):
    ah, al = _split2(a)
    bh, bl = _split2(b)
    return _dot(ah, bh) + (_dot(ah, bl) + _dot(al, bh))


def _softplus(x):
    return jnp.maximum(x, 0.0) + jnp.log(1.0 + jnp.exp(-jnp.abs(x)))


def _params(*sem):
    return pltpu.CompilerParams(dimension_semantics=sem, vmem_limit_bytes=VMEM_LIMIT)


def _ada_kernel(c_ref, w_ref, b_ref, o_ref):
    c = c_ref[...]
    cond = c * jax.nn.sigmoid(c)
    o_ref[0] = _dot3(cond, w_ref[0]) + b_ref[0]


def _ada_mod(c, ada_w, ada_b):
    depth, d, d6 = ada_w.shape
    bsz = c.shape[0]
    out = pl.pallas_call(
        _ada_kernel,
        grid=(depth, d6 // d),
        in_specs=[
            pl.BlockSpec((bsz, d), lambda l, j: (0, 0)),
            pl.BlockSpec((1, d, d), lambda l, j: (l, 0, j)),
            pl.BlockSpec((1, 1, d), lambda l, j: (l, 0, j)),
        ],
        out_specs=pl.BlockSpec((1, bsz, d), lambda l, j: (l, 0, j)),
        out_shape=jax.ShapeDtypeStruct((depth, bsz, d6), F32),
        compiler_params=_params("arbitrary", "arbitrary"),
        name="ada_mod",
    )(c, ada_w, ada_b.reshape(depth, 1, d6))
    return out.reshape(depth, bsz, d6 // d, d)


def _in_proj_kernel(c_rwkv, d_sb, x_ref, mod_ref, g_ref, w_ref, bdm_ref, qg_ref, kg_ref,
                    zr_ref, q_ref, k_ref, v_ref):
    x = x_ref[...]
    ms = jnp.mean(x * x, axis=-1, keepdims=True)
    h = x * lax.rsqrt(ms + NORM_EPS) * g_ref[...]
    h = h * (1.0 + mod_ref[0, 1:2, :]) + mod_ref[0, 0:1, :]
    z = _dot(h.astype(BF), w_ref[...])
    zr_ref[...] = z[:, :c_rwkv]
    bdm = bdm_ref[...]
    scale = 1.0 / math.sqrt(HEAD_DIM)
    for p in range(d_sb // LANES):
        lo = p * LANES
        q = z[:, c_rwkv + lo:c_rwkv + lo + LANES]
        k = z[:, c_rwkv + d_sb + lo:c_rwkv + d_sb + lo + LANES]
        qn = q * lax.rsqrt(_dot_lhs2(q * q, bdm) + NORM_EPS) * qg_ref[...]
        kn = k * lax.rsqrt(_dot_lhs2(k * k, bdm) + NORM_EPS) * kg_ref[...]
        q_ref[:, lo:lo + LANES] = (qn * scale).astype(BF)
        k_ref[:, lo:lo + LANES] = kn.astype(BF)
    v_ref[...] = z[:, c_rwkv + 2 * d_sb:].astype(BF)


def _in_proj(x2, mod, g, w_bf, bdm, qg, kg, seq, c_rwkv, d_sb):
    n, d = x2.shape
    c_in = w_bf.shape[1]
    tm = ROW_TILE
    per_seq = seq // tm
    row = lambda i: (i, 0)
    const = lambda i: (0, 0)
    return pl.pallas_call(
        functools.partial(_in_proj_kernel, c_rwkv, d_sb),
        grid=(n // tm,),
        in_specs=[
            pl.BlockSpec((tm, d), row),
            pl.BlockSpec((1,) + mod.shape[1:], lambda i: (i // per_seq, 0, 0)),
            pl.BlockSpec((1, d), const),
            pl.BlockSpec((d, c_in), const),
            pl.BlockSpec((LANES, LANES), const),
            pl.BlockSpec((1, LANES), const),
            pl.BlockSpec((1, LANES), const),
        ],
        out_specs=[
            pl.BlockSpec((tm, c_rwkv), row),
            pl.BlockSpec((tm, d_sb), row),
            pl.BlockSpec((tm, d_sb), row),
            pl.BlockSpec((tm, d_sb), row),
        ],
        out_shape=[
            jax.ShapeDtypeStruct((n, c_rwkv), F32),
            jax.ShapeDtypeStruct((n, d_sb), BF),
            jax.ShapeDtypeStruct((n, d_sb), BF),
            jax.ShapeDtypeStruct((n, d_sb), BF),
        ],
        compiler_params=_params("arbitrary"),
        name="in_proj",
    )(x2, mod, g, w_bf, bdm, qg, kg)


def _rwkv_kernel(has_vres, d_r, *refs):
    if has_vres:
        (z_ref, vf_ref, mu_ref, w0_ref, a0_ref, lora_ref, gup_ref, kk_ref, ka_ref, rk_ref, lnw_ref,
         lnb_ref, vd_ref, vu_ref, vb_ref, tri_ref, bdm_ref, o_ref, st_ref, prev_ref) = refs
    else:
        (z_ref, mu_ref, w0_ref, a0_ref, lora_ref, gup_ref, kk_ref, ka_ref, rk_ref, lnw_ref,
         lnb_ref, tri_ref, bdm_ref, o_ref, vo_ref, st_ref, prev_ref) = refs
    chunk = RWKV_CHUNK
    two = 2 * chunk
    nb = z_ref.shape[0]
    groups = d_r // LANES

    @pl.when(pl.program_id(1) == 0)
    def _():
        st_ref[...] = jnp.zeros_like(st_ref)
        prev_ref[...] = jnp.zeros_like(prev_ref)

    first = lax.broadcasted_iota(jnp.int32, (chunk, z_ref.shape[2]), 0) == 0
    mu = mu_ref[...]
    zs_parts = []
    for n in range(nb):
        z = z_ref[n]
        z_prev = jnp.where(first, prev_ref[n, 0:1, :], pltpu.roll(z, 1, axis=0))
        prev_ref[n, 0:1, :] = z[chunk - 1:chunk, :]
        zs_parts.append(z + (z_prev - z) * mu)
    zs = jnp.concatenate(zs_parts, axis=0)

    r = zs[:, :d_r]
    k = zs[:, d_r:2 * d_r]
    v = zs[:, 2 * d_r:3 * d_r]
    wa = zs[:, 3 * d_r:3 * d_r + LANES]
    g_lo = zs[:, 3 * d_r + LANES:]
    lane = lax.broadcasted_iota(jnp.int32, wa.shape, 1)
    feed = jnp.where(lane < DECAY_LORA, jnp.tanh(wa), wa)
    lora = _dot(feed.astype(BF), lora_ref[...])
    w_pre = w0_ref[...] + lora[:, :d_r]
    lw = -jnp.exp(-_softplus(-w_pre) - 0.5)
    a = jax.nn.sigmoid(a0_ref[...] + lora[:, d_r:])
    g = _dot(jax.nn.sigmoid(g_lo).astype(BF), gup_ref[...])
    if has_vres:
        vf = jnp.concatenate([vf_ref[n] for n in range(nb)], axis=0)
        low = _dot(v.astype(BF), vd_ref[...])
        v = v + (vf - v) * jax.nn.sigmoid(vb_ref[...] + _dot(low.astype(BF), vu_ref[...]))
    else:
        for n in range(nb):
            vo_ref[n] = v[n * chunk:(n + 1) * chunk]

    bdm = bdm_ref[...]
    kk = k * kk_ref[...]
    ssq = jnp.concatenate(
        [_dot_lhs2(jnp.square(kk[:, i * LANES:(i + 1) * LANES]), bdm) for i in range(groups)],
        axis=1) * float(HEAD_DIM)
    kk = kk / jnp.maximum(jnp.sqrt(ssq), L2_EPS)
    k = k * (1.0 + (a - 1.0) * ka_ref[...])

    tri = tri_ref[...]
    l1 = lw.astype(BF)
    rem = lw - l1.astype(F32)
    l2 = rem.astype(BF)
    l3 = (rem - l2.astype(F32)).astype(BF)
    cum_parts, total_parts = [], []
    for n in range(nb):
        rows = slice(n * chunk, (n + 1) * chunk)
        c = _dot(tri, l1[rows]) + (_dot(tri, l2[rows]) + _dot(tri, l3[rows]))
        cum_parts.append(c)
        total_parts.append(jnp.broadcast_to(c[chunk - 1:chunk, :], c.shape))
    cum = jnp.concatenate(cum_parts, axis=0)
    total = jnp.concatenate(total_parts, axis=0)
    grow = jnp.exp(-cum)
    to_end = jnp.exp(total - cum)
    b = kk * a
    a_src = -(kk * jnp.exp(cum - lw))
    r_src = r * jnp.exp(cum)
    b_src = b * grow
    k_src = k * grow
    b_end = b * to_end
    k_end = k * to_end
    end_decay = jnp.exp(total)

    row = lax.broadcasted_iota(jnp.int32, (two, LANES), 0)
    col = lax.broadcasted_iota(jnp.int32, (two, LANES), 1)
    same_head = (row < chunk) == (col < HEAD_DIM)
    row_in = row & (chunk - 1)
    col_in = col & (chunk - 1)
    strict = row_in > col_in
    incl = row_in >= col_in
    eye = (row == col).astype(F32)

    chains = [(n, p) for n in range(nb) for p in range(groups)]

    def part(t, n, p):
        return t[n * chunk:(n + 1) * chunk, p * LANES:(p + 1) * LANES]

    def bd(t, n, p):
        s = part(t, n, p)
        return jnp.where(same_head, jnp.concatenate([s, s], axis=0), 0.0)

    a_s = [bd(a_src, n, p).astype(BF) for n, p in chains]
    r_s = [bd(r_src, n, p).astype(BF) for n, p in chains]
    b_s = [bd(b_src, n, p).astype(BF) for n, p in chains]
    k_s = [bd(k_src, n, p).astype(BF) for n, p in chains]
    v_bd = [bd(v, n, p).astype(BF) for n, p in chains]
    st = [st_ref[n, p] for n, p in chains]
    st_b = [s.astype(BF) for s in st]

    pair = [_dot_nt(jnp.concatenate([a_s[c], r_s[c]], axis=0),
                    jnp.concatenate([b_s[c], k_s[c]], axis=0)) for c in range(len(chains))]
    n_ab = [jnp.where(strict, m[:two, :two], 0.0) for m in pair]
    a_ak = [jnp.where(strict, m[:two, two:], 0.0).astype(BF) for m in pair]
    a_rb = [jnp.where(incl, m[two:, :two], 0.0).astype(BF) for m in pair]
    a_rk = [jnp.where(incl, m[two:, two:], 0.0).astype(BF) for m in pair]

    inv = [eye + m for m in n_ab]
    power = [m.astype(BF) for m in n_ab]
    for _ in range(int(math.log2(chunk)) - 1):
        power = [_dot(m, m).astype(BF) for m in power]
        inv = [i + _dot(i.astype(BF), m) for i, m in zip(inv, power)]

    x0 = [_dot(jnp.concatenate([a_s[c], a_ak[c]], axis=1),
               jnp.concatenate([st_b[c], v_bd[c]], axis=0)) for c in range(len(chains))]
    u_b = [_dot(inv[c].astype(BF), x0[c].astype(BF)).astype(BF) for c in range(len(chains))]
    y_bd = [_dot(jnp.concatenate([r_s[c], a_rb[c], a_rk[c]], axis=1),
                 jnp.concatenate([st_b[c], u_b[c], v_bd[c]], axis=0)) for c in range(len(chains))]
    for c, (n, p) in enumerate(chains):
        bk_t = jnp.concatenate([bd(b_end, n, p), bd(k_end, n, p)], axis=0).T.astype(BF)
        decay_col = jnp.concatenate([part(end_decay, n, p)] * 2, axis=0).T
        st_ref[n, p] = decay_col * st[c] + _dot(bk_t, jnp.concatenate([u_b[c], v_bd[c]], axis=0))

    y = jnp.concatenate(
        [jnp.concatenate([y_bd[n * groups + p][:chunk] + y_bd[n * groups + p][chunk:]
                          for p in range(groups)], axis=1) for n in range(nb)], axis=0)
    rkr = r * k * rk_ref[...]
    outs = []
    for p in range(groups):
        sl = slice(p * LANES, (p + 1) * LANES)
        yp = y[:, sl]
        cen = yp - _dot_lhs2(yp, bdm)
        var = _dot_lhs2(cen * cen, bdm)
        yn = cen * lax.rsqrt(var + GN_EPS) * lnw_ref[:, sl] + lnb_ref[:, sl]
        bonus = _dot_lhs2(rkr[:, sl], bdm) * float(HEAD_DIM) * v[:, sl]
        outs.append((yn + bonus) * g[:, sl])
    out = jnp.concatenate(outs, axis=1).astype(o_ref.dtype)
    for n in range(nb):
        o_ref[n] = out[n * chunk:(n + 1) * chunk]


def _rwkv_mix(z, v_first, mu, w0, a0, lora_up, gate_up, k_k, k_a, rk, lnw, lnb, vres, tri, bdm):
    bsz, seq, c_rwkv = z.shape
    d_r = w0.shape[1]
    chunk = RWKV_CHUNK
    nb = RWKV_SEQS
    has_vres = vres is not None
    tok_z = pl.BlockSpec((nb, chunk, c_rwkv), lambda b, c: (b, c, 0))
    tok = pl.BlockSpec((nb, chunk, d_r), lambda b, c: (b, c, 0))
    full = lambda arr: pl.BlockSpec(arr.shape, lambda b, c: (0,) * arr.ndim)
    consts = [mu, w0, a0, lora_up, gate_up, k_k, k_a, rk, lnw, lnb]
    args = [z] + ([v_first] if has_vres else []) + consts + (list(vres) if has_vres else []) + [tri, bdm]
    in_specs = ([tok_z] + ([tok] if has_vres else []) + [full(t) for t in consts]
                + ([full(t) for t in vres] if has_vres else []) + [full(tri), full(bdm)])
    out_specs = [tok] if has_vres else [tok, tok]
    out_shape = [jax.ShapeDtypeStruct((bsz, seq, d_r), BF)]
    if not has_vres:
        out_shape.append(jax.ShapeDtypeStruct((bsz, seq, d_r), F32))
    res = pl.pallas_call(
        functools.partial(_rwkv_kernel, has_vres, d_r),
        grid=(bsz // nb, seq // chunk),
        in_specs=in_specs,
        out_specs=out_specs,
        out_shape=out_shape,
        scratch_shapes=[pltpu.VMEM((nb, d_r // LANES, LANES, LANES), F32),
                        pltpu.VMEM((nb, 8, c_rwkv), F32)],
        compiler_params=_params("arbitrary", "arbitrary"),
        name="rwkv_mix",
    )(*args)
    return (res[0], v_first) if has_vres else (res[0], res[1])


def _sb_kernel(q_ref, k_ref, v_ref, og_ref, tri_ref, bdm_ref, o_ref, acc_ref, carry_ref):
    t = SB_TILE
    qi = pl.program_id(2)
    q = q_ref[0]
    lane = lax.broadcasted_iota(jnp.int32, (t, LANES), 1)
    rowi = lax.broadcasted_iota(jnp.int32, (t, t), 0)
    coli = lax.broadcasted_iota(jnp.int32, (t, t), 1)
    causal = coli < rowi
    zero = jnp.zeros_like(q)
    q_heads = [jnp.where(lane < HEAD_DIM, q, zero), jnp.where(lane >= HEAD_DIM, q, zero)]
    tri = tri_ref[...]

    def block(j, diagonal):
        start = pl.multiple_of(j * t, t)
        kb = k_ref[0, pl.ds(start, t), :]
        vb = v_ref[0, pl.ds(start, t), :]
        logits = [_dot_nt(q_heads[h], kb) for h in range(2)]
        soft = [_softplus(s) for s in logits]
        if diagonal:
            drop = [jnp.where(causal, s, 0.0) for s in soft]
        else:
            drop = soft
        sums = [_dot(d.astype(BF), tri) for d in drop]
        for h in range(2):
            arg = logits[h] - soft[h] - sums[h][:, :t]
            if diagonal:
                w = jnp.where(causal, jnp.exp(arg), 0.0)
                acc_ref[h] = _dot(w.astype(BF), vb)
                carry_ref[h] = sums[h][:, t:]
            else:
                carry = carry_ref[h]
                w = jnp.exp(arg - jnp.concatenate([carry] * (t // LANES), axis=1))
                acc_ref[h] += _dot(w.astype(BF), vb)
                carry_ref[h] = carry + sums[h][:, t:]

    block(qi, True)

    def body(i, c):
        block(qi - 1 - i, False)
        return c

    lax.fori_loop(0, qi, body, 0)

    o = jnp.where(lane < HEAD_DIM, acc_ref[0], acc_ref[1])
    ms = _dot_lhs2(o * o, bdm_ref[...])
    o_ref[0] = (o * lax.rsqrt(ms + NORM_EPS) * og_ref[...]).astype(o_ref.dtype)


def _sb_attention(q, k, v, og, tri2, bdm):
    bsz, seq, ds = q.shape
    t = SB_TILE
    return pl.pallas_call(
        _sb_kernel,
        grid=(bsz, ds // LANES, seq // t),
        in_specs=[
            pl.BlockSpec((1, t, LANES), lambda b, p, i: (b, i, p)),
            pl.BlockSpec((1, seq, LANES), lambda b, p, i: (b, 0, p)),
            pl.BlockSpec((1, seq, LANES), lambda b, p, i: (b, 0, p)),
            pl.BlockSpec((1, LANES), lambda b, p, i: (0, p)),
            pl.BlockSpec((t, t + LANES), lambda b, p, i: (0, 0)),
            pl.BlockSpec((LANES, LANES), lambda b, p, i: (0, 0)),
        ],
        out_specs=pl.BlockSpec((1, t, LANES), lambda b, p, i: (b, i, p)),
        out_shape=jax.ShapeDtypeStruct((bsz, seq, ds), BF),
        scratch_shapes=[pltpu.VMEM((2, t, LANES), F32), pltpu.VMEM((2, t, LANES), F32)],
        compiler_params=_params("arbitrary", "arbitrary", "arbitrary"),
        name="sb_attention",
    )(q, k, v, og, tri2, bdm)


def _out_proj_kernel(yr_ref, ys_ref, x_ref, mod_ref, g_ref, wr_ref, ws_ref, rwh_ref, rwl_ref, rb_ref,
                     xo_ref, h_ref, idx_ref, gate_ref):
    mixed = _dot(yr_ref[...], wr_ref[...]) + _dot(ys_ref[...], ws_ref[...])
    x = x_ref[...] + mod_ref[0, 2:3, :] * mixed
    xo_ref[...] = x
    ms = jnp.mean(x * x, axis=-1, keepdims=True)
    h = x * lax.rsqrt(ms + NORM_EPS) * g_ref[...]
    h = h * (1.0 + mod_ref[0, 4:5, :]) + mod_ref[0, 3:4, :]
    h_ref[...] = h.astype(BF)

    hh, hl = _split2(h)
    rwh = rwh_ref[...]
    logits = _dot_nt(rwh, hh) + (_dot_nt(rwh, hl) + _dot_nt(rwl_ref[...], hh)) + rb_ref[...]
    eidx = lax.broadcasted_iota(jnp.int32, logits.shape, 0)
    tops, idxs = [], []
    for _ in range(TOP_K):
        m = jnp.max(logits, axis=0, keepdims=True)
        idx = jnp.min(jnp.where(logits == m, eidx, N_EXPERTS), axis=0, keepdims=True)
        tops.append(m)
        idxs.append(idx)
        logits = jnp.where(eidx == idx, -jnp.inf, logits)
    exps = [jnp.exp(m - tops[0]) for m in tops]
    den = exps[0] + exps[1] + exps[2] + exps[3]
    pad = [jnp.zeros_like(den)] * (8 - TOP_K)
    gate_ref[...] = jnp.concatenate([e / den for e in exps] + pad, axis=0)
    idx_ref[...] = jnp.concatenate(idxs + [jnp.zeros_like(idxs[0])] * (8 - TOP_K), axis=0)


def _out_proj(yr, ys, x2, mod, g, wr, ws, rwh, rwl, rb, seq):
    n, d = x2.shape
    dr = yr.shape[1]
    ds = ys.shape[1]
    tm = ROW_TILE
    per_seq = seq // tm
    row = lambda i: (i, 0)
    const = lambda i: (0, 0)
    colblk = lambda i: (0, i)
    return pl.pallas_call(
        _out_proj_kernel,
        grid=(n // tm,),
        in_specs=[
            pl.BlockSpec((tm, dr), row),
            pl.BlockSpec((tm, ds), row),
            pl.BlockSpec((tm, d), row),
            pl.BlockSpec((1,) + mod.shape[1:], lambda i: (i // per_seq, 0, 0)),
            pl.BlockSpec((1, d), const),
            pl.BlockSpec((dr, d), const),
            pl.BlockSpec((ds, d), const),
            pl.BlockSpec((N_EXPERTS, d), const),
            pl.BlockSpec((N_EXPERTS, d), const),
            pl.BlockSpec((N_EXPERTS, 1), const),
        ],
        out_specs=[
            pl.BlockSpec((tm, d), row),
            pl.BlockSpec((tm, d), row),
            pl.BlockSpec((8, tm), colblk),
            pl.BlockSpec((8, tm), colblk),
        ],
        out_shape=[
            jax.ShapeDtypeStruct((n, d), F32),
            jax.ShapeDtypeStruct((n, d), BF),
            jax.ShapeDtypeStruct((8, n), jnp.int32),
            jax.ShapeDtypeStruct((8, n), F32),
        ],
        compiler_params=_params("arbitrary"),
        name="out_proj_router",
    )(yr, ys, x2, mod, g, wr, ws, rwh, rwl, rb)


def _moe_kernel(te_ref, nu_ref, x_ref, w1_ref, b1_ref, w2_ref, b2_ref, o_ref):
    i = pl.program_id(0)
    f = w2_ref.shape[1]

    @pl.when(i < nu_ref[0])
    def _():
        hid = _dot(x_ref[...], w1_ref[0]) + b1_ref[0]
        glu = jnp.minimum(hid[:, :f], SWIGLU_LIMIT)
        lin = jnp.clip(hid[:, f:], -SWIGLU_LIMIT, SWIGLU_LIMIT)
        act = glu * jax.nn.sigmoid(SWIGLU_ALPHA * glu) * (lin + 1.0)
        o_ref[...] = (_dot(act.astype(BF), w2_ref[0]) + b2_ref[0]).astype(o_ref.dtype)

    @pl.when(i >= nu_ref[0])
    def _():
        o_ref[...] = jnp.zeros_like(o_ref)


def _moe_experts(tile_expert, n_used, xs, w1, b1, w2, b2):
    n_rows, d = xs.shape
    f2 = w1.shape[2]
    f = w2.shape[1]
    tm = MOE_TILE
    return pl.pallas_call(
        _moe_kernel,
        grid_spec=pltpu.PrefetchScalarGridSpec(
            num_scalar_prefetch=2,
            grid=(n_rows // tm,),
            in_specs=[
                pl.BlockSpec((tm, d), lambda i, te, nu: (i, 0)),
                pl.BlockSpec((1, d, f2), lambda i, te, nu: (te[i], 0, 0)),
                pl.BlockSpec((1, 1, f2), lambda i, te, nu: (te[i], 0, 0)),
                pl.BlockSpec((1, f, d), lambda i, te, nu: (te[i], 0, 0)),
                pl.BlockSpec((1, 1, d), lambda i, te, nu: (te[i], 0, 0)),
            ],
            out_specs=pl.BlockSpec((tm, d), lambda i, te, nu: (i, 0)),
        ),
        out_shape=jax.ShapeDtypeStruct((n_rows, d), F32),
        compiler_params=_params("arbitrary"),
        name="moe_experts",
    )(tile_expert, n_used, xs, w1, b1.reshape(b1.shape[0], 1, f2), w2, b2.reshape(b2.shape[0], 1, d))


def _moe_ffn(h_bf, top_idx, top_gate, w1, b1, w2, b2):
    n_tok, d = h_bf.shape
    tm = MOE_TILE
    n_assign = n_tok * TOP_K
    n_rows = n_assign + N_EXPERTS * tm
    n_tiles = n_rows // tm
    flat_e = top_idx.reshape(-1)
    order = jnp.argsort(flat_e, stable=True)
    rank_of = jnp.argsort(order)
    counts = jnp.sum((flat_e[:, None] == jnp.arange(N_EXPERTS, dtype=jnp.int32)[None, :]).astype(jnp.int32),
                     axis=0)
    padded = (counts + tm - 1) // tm * tm
    padded_end = jnp.cumsum(padded)
    padded_start = padded_end - padded
    start = jnp.cumsum(counts) - counts
    shift = padded_start - start
    pos = (rank_of + shift[flat_e]).reshape(TOP_K, n_tok)
    tile_start = jnp.arange(n_tiles, dtype=jnp.int32) * tm
    tile_expert = jnp.minimum(
        jnp.sum((tile_start[:, None] >= padded_end[None, :]).astype(jnp.int32), axis=1),
        N_EXPERTS - 1).astype(jnp.int32)
    n_used = (padded_end[-1:] // tm).astype(jnp.int32)
    rows = jnp.arange(n_rows, dtype=jnp.int32)
    row_e = jnp.repeat(tile_expert, tm)
    local = rows - padded_start[row_e]
    src = jnp.clip(start[row_e] + local, 0, n_assign - 1)
    row_tok = jnp.where(local < counts[row_e], order[src] % n_tok, 0)
    xs = h_bf[row_tok]
    ys = _moe_experts(tile_expert, n_used, xs, w1, b1, w2, b2)
    out = ys[pos[0]] * top_gate[0][:, None]
    for s in range(1, TOP_K):
        out = out + ys[pos[s]] * top_gate[s][:, None]
    return out


def _block_diag_mean():
    i = jnp.arange(LANES)
    return jnp.where((i[:, None] // HEAD_DIM) == (i[None, :] // HEAD_DIM), 1.0 / HEAD_DIM, 0.0).astype(BF)


def _block_diag2(a, b):
    top = jnp.concatenate([a, jnp.zeros((a.shape[0], b.shape[1]), a.dtype)], axis=1)
    bot = jnp.concatenate([jnp.zeros((b.shape[0], a.shape[1]), b.dtype), b], axis=1)
    return jnp.concatenate([top, bot], axis=0)


@jax.jit
def _forward(x, c, norm1_g, norm2_g, ada_w, ada_b, w_in, shift_mu, decay_w0, decay_up, iclr_a0,
             iclr_up, gate_up, k_k, k_a, r_k, lnx_w, lnx_b, vres_down, vres_up, vres_b, q_norm_g,
             k_norm_g, sb_out_g, w_out, router_w, router_b, exp_w1, exp_b1, exp_w2, exp_b2):
    bsz, seq, d = x.shape
    depth = w_in.shape[0]
    d_r = decay_w0.shape[1]
    c_rwkv = shift_mu.shape[1]
    d_sb = sb_out_g.shape[1]
    n_tok = bsz * seq
    assert seq % ROW_TILE == 0 and seq % SB_TILE == 0 and seq % RWKV_CHUNK == 0
    assert bsz % RWKV_SEQS == 0 and DECAY_LORA + ICLR_LORA == LANES and GATE_LORA == LANES

    bdm = _block_diag_mean()
    ci = jnp.arange(RWKV_CHUNK)
    tri_chunk = (ci[None, :] <= ci[:, None]).astype(BF)
    si = jnp.arange(SB_TILE)
    tri_sb = jnp.concatenate([(si[:, None] > si[None, :]).astype(BF),
                              jnp.ones((SB_TILE, LANES), BF)], axis=1)
    row_vec = lambda t: t.reshape(1, -1)

    mod_all = _ada_mod(c, ada_w, ada_b)
    x2 = x.reshape(n_tok, d)
    v_first = None
    for l in range(depth):
        mod = mod_all[l]
        zr, q, k, v = _in_proj(
            x2, mod, row_vec(norm1_g[l]), w_in[l].astype(BF), bdm,
            row_vec(jnp.tile(q_norm_g[l], LANES // HEAD_DIM)),
            row_vec(jnp.tile(k_norm_g[l], LANES // HEAD_DIM)), seq, c_rwkv, d_sb)
        if l == 0:
            vres = None
        else:
            rank = vres_down.shape[2]
            vres = (jnp.pad(vres_down[l - 1], ((0, 0), (0, LANES - rank))).astype(BF),
                    jnp.pad(vres_up[l - 1], ((0, LANES - rank), (0, 0))).astype(BF),
                    row_vec(vres_b[l - 1]))
        y_r, v_first = _rwkv_mix(
            zr.reshape(bsz, seq, c_rwkv), v_first, row_vec(shift_mu[l]), row_vec(decay_w0[l]),
            row_vec(iclr_a0[l]), _block_diag2(decay_up[l], iclr_up[l]).astype(BF),
            gate_up[l].astype(BF), row_vec(k_k[l]), row_vec(k_a[l]), row_vec(r_k[l]),
            row_vec(lnx_w[l]), row_vec(lnx_b[l]), vres, tri_chunk, bdm)
        y_s = _sb_attention(q.reshape(bsz, seq, d_sb), k.reshape(bsz, seq, d_sb),
                            v.reshape(bsz, seq, d_sb), row_vec(sb_out_g[l]), tri_sb, bdm)
        rw_t = router_w[l].T
        rwh = rw_t.astype(BF)
        rwl = (rw_t - rwh.astype(F32)).astype(BF)
        w_o = w_out[l].astype(BF)
        x_mid, h2, top_idx, top_gate = _out_proj(
            y_r.reshape(n_tok, d_r), y_s.reshape(n_tok, d_sb), x2, mod, row_vec(norm2_g[l]),
            w_o[:d_r], w_o[d_r:], rwh, rwl, router_b[l].reshape(N_EXPERTS, 1), seq)
        moe = _moe_ffn(h2, top_idx[:TOP_K], top_gate[:TOP_K], exp_w1[l].astype(BF), exp_b1[l],
                       exp_w2[l].astype(BF), exp_b2[l])
        gate2 = jnp.repeat(mod[:, 5, :], seq, axis=0)
        x2 = x_mid + gate2 * moe
    return x2.reshape(bsz, seq, d)


def kernel(x, c, norm1_g, norm2_g, ada_w, ada_b, w_in, shift_mu, decay_w0, decay_up, iclr_a0, iclr_up,
           gate_up, k_k, k_a, r_k, lnx_w, lnx_b, vres_down, vres_up, vres_b, q_norm_g, k_norm_g,
           sb_out_g, w_out, router_w, router_b, exp_w1, exp_b1, exp_w2, exp_b2):
    return _forward(x, c, norm1_g, norm2_g, ada_w, ada_b, w_in, shift_mu, decay_w0, decay_up, iclr_a0,
                    iclr_up, gate_up, k_k, k_a, r_k, lnx_w, lnx_b, vres_down, vres_up, vres_b,
                    q_norm_g, k_norm_g, sb_out_g, w_out, router_w, router_b, exp_w1, exp_b1,
                    exp_w2, exp_b2)
```

```python
import functools
import math

import jax
import jax.numpy as jnp
from jax import lax
from jax.experimental import pallas as pl
from jax.experimental.pallas import tpu as pltpu

F32 = jnp.float32
BF = jnp.bfloat16

HEAD_DIM = 64
LANES = 128
RWKV_CHUNK = 64
RWKV_SEQS = 2
DECAY_LORA = 64
ICLR_LORA = 64
GATE_LORA = 128
N_EXPERTS = 32
TOP_K = 4
SWIGLU_ALPHA = 1.702
SWIGLU_LIMIT = 7.0
NORM_EPS = 1e-6
GN_EPS = 1e-5 * HEAD_DIM
L2_EPS = 1e-12

ROW_TILE = 256
SB_TILE = 256
MOE_TILE = 512
VMEM_LIMIT = 48 * 1024 * 1024
MOE_VMEM_LIMIT = 58 * 1024 * 1024
LOG2E = 1.4426950408889634
SB_DEAD_MASS = 150.0


def _pack_bf16_pairs(x):
    n = x.shape[1] // 2
    lo = lax.bitcast_convert_type(x[:, :n].astype(BF).astype(F32), jnp.uint32)
    hi = lax.bitcast_convert_type(x[:, n:].astype(BF).astype(F32), jnp.uint32)
    return hi | (lo >> 16)


def _unpack_bf16_pairs(w):
    lo = lax.bitcast_convert_type(w << 16, F32)
    hi = lax.bitcast_convert_type(w & jnp.uint32(0xFFFF0000), F32)
    return lo, hi


def _dot(a, b):
    return jnp.dot(a, b, preferred_element_type=F32)


def _dot_nt(a, b):
    return lax.dot_general(a, b, (((1,), (1,)), ((), ())), preferred_element_type=F32)


def _split2(x):
    hi = x.astype(BF)
    lo = (x - hi.astype(F32)).astype(BF)
    return hi, lo


def _dot_lhs2(x, m_bf):
    hi, lo = _split2(x)
    return _dot(hi, m_bf) + _dot(lo, m_bf)


def _dot3(a, b):
    ah, al = _split2(a)
    bh, bl = _split2(b)
    return _dot(ah, bh) + (_dot(ah, bl) + _dot(al, bh))


def _softplus(x):
    return jnp.maximum(x, 0.0) + jnp.log(1.0 + jnp.exp(-jnp.abs(x)))


def _softplus2(x):
    return jnp.maximum(x, 0.0) + jnp.log2(1.0 + jnp.exp2(-jnp.abs(x)))


def _params(*sem):
    return pltpu.CompilerParams(dimension_semantics=sem, vmem_limit_bytes=VMEM_LIMIT)


def _ada_kernel(c_ref, w_ref, b_ref, o_ref):
    c = c_ref[...]
    cond = c * jax.nn.sigmoid(c)
    o_ref[0] = _dot3(cond, w_ref[0]) + b_ref[0]


def _ada_mod(c, ada_w, ada_b):
    depth, d, d6 = ada_w.shape
    bsz = c.shape[0]
    out = pl.pallas_call(
        _ada_kernel,
        grid=(depth, d6 // d),
        in_specs=[
            pl.BlockSpec((bsz, d), lambda l, j: (0, 0)),
            pl.BlockSpec((1, d, d), lambda l, j: (l, 0, j)),
            pl.BlockSpec((1, 1, d), lambda l, j: (l, 0, j)),
        ],
        out_specs=pl.BlockSpec((1, bsz, d), lambda l, j: (l, 0, j)),
        out_shape=jax.ShapeDtypeStruct((depth, bsz, d6), F32),
        compiler_params=_params("arbitrary", "arbitrary"),
        name="ada_mod",
    )(c, ada_w, ada_b.reshape(depth, 1, d6))
    return out.reshape(depth, bsz, d6 // d, d)


def _in_proj_kernel(c_rwkv, d_sb, x_ref, mod_ref, g_ref, w_ref, bdm_ref, qg_ref, kg_ref,
                    zr_ref, q_ref, k_ref, v_ref):
    x = x_ref[...]
    ms = jnp.mean(x * x, axis=-1, keepdims=True)
    h = x * lax.rsqrt(ms + NORM_EPS) * g_ref[...]
    h = h * (1.0 + mod_ref[0, 1:2, :]) + mod_ref[0, 0:1, :]
    z = _dot(h.astype(BF), w_ref[...])
    zr_ref[...] = z[:, :c_rwkv]
    bdm = bdm_ref[...]
    scale = LOG2E / math.sqrt(HEAD_DIM)
    for p in range(d_sb // LANES):
        lo = p * LANES
        q = z[:, c_rwkv + lo:c_rwkv + lo + LANES]
        k = z[:, c_rwkv + d_sb + lo:c_rwkv + d_sb + lo + LANES]
        qn = q * lax.rsqrt(_dot_lhs2(q * q, bdm) + NORM_EPS) * qg_ref[...]
        kn = k * lax.rsqrt(_dot_lhs2(k * k, bdm) + NORM_EPS) * kg_ref[...]
        q_ref[:, lo:lo + LANES] = (qn * scale).astype(BF)
        k_ref[:, lo:lo + LANES] = kn.astype(BF)
    v_ref[...] = z[:, c_rwkv + 2 * d_sb:].astype(BF)


def _in_proj(x2, mod, g, w_bf, bdm, qg, kg, seq, c_rwkv, d_sb):
    n, d = x2.shape
    c_in = w_bf.shape[1]
    tm = ROW_TILE
    per_seq = seq // tm
    row = lambda i: (i, 0)
    const = lambda i: (0, 0)
    return pl.pallas_call(
        functools.partial(_in_proj_kernel, c_rwkv, d_sb),
        grid=(n // tm,),
        in_specs=[
            pl.BlockSpec((tm, d), row),
            pl.BlockSpec((1,) + mod.shape[1:], lambda i: (i // per_seq, 0, 0)),
            pl.BlockSpec((1, d), const),
            pl.BlockSpec((d, c_in), const),
            pl.BlockSpec((LANES, LANES), const),
            pl.BlockSpec((1, LANES), const),
            pl.BlockSpec((1, LANES), const),
        ],
        out_specs=[
            pl.BlockSpec((tm, c_rwkv), row),
            pl.BlockSpec((tm, d_sb), row),
            pl.BlockSpec((tm, d_sb), row),
            pl.BlockSpec((tm, d_sb), row),
        ],
        out_shape=[
            jax.ShapeDtypeStruct((n, c_rwkv), F32),
            jax.ShapeDtypeStruct((n, d_sb), BF),
            jax.ShapeDtypeStruct((n, d_sb), BF),
            jax.ShapeDtypeStruct((n, d_sb), BF),
        ],
        compiler_params=_params("arbitrary"),
        name="in_proj",
    )(x2, mod, g, w_bf, bdm, qg, kg)


def _rwkv_kernel(has_vres, d_r, *refs):
    if has_vres:
        (z_ref, vf_ref, mu_ref, w0_ref, a0_ref, lora_ref, gup_ref, kk_ref, ka_ref, rk_ref, lnw_ref,
         lnb_ref, vd_ref, vu_ref, vb_ref, tri_ref, bdm_ref, o_ref, st_ref, prev_ref) = refs
    else:
        (z_ref, mu_ref, w0_ref, a0_ref, lora_ref, gup_ref, kk_ref, ka_ref, rk_ref, lnw_ref,
         lnb_ref, tri_ref, bdm_ref, o_ref, vo_ref, st_ref, prev_ref) = refs
    chunk = RWKV_CHUNK
    two = 2 * chunk
    nb = z_ref.shape[0]
    groups = d_r // LANES

    @pl.when(pl.program_id(1) == 0)
    def _():
        st_ref[...] = jnp.zeros_like(st_ref)
        prev_ref[...] = jnp.zeros_like(prev_ref)

    first = lax.broadcasted_iota(jnp.int32, (chunk, z_ref.shape[2]), 0) == 0
    mu = mu_ref[...]
    zs_parts = []
    for n in range(nb):
        z = z_ref[n]
        z_prev = jnp.where(first, prev_ref[n, 0:1, :], pltpu.roll(z, 1, axis=0))
        prev_ref[n, 0:1, :] = z[chunk - 1:chunk, :]
        zs_parts.append(z + (z_prev - z) * mu)
    zs = jnp.concatenate(zs_parts, axis=0)

    r = zs[:, :d_r]
    k = zs[:, d_r:2 * d_r]
    v = zs[:, 2 * d_r:3 * d_r]
    wa = zs[:, 3 * d_r:3 * d_r + LANES]
    g_lo = zs[:, 3 * d_r + LANES:]
    lane = lax.broadcasted_iota(jnp.int32, wa.shape, 1)
    feed = jnp.where(lane < DECAY_LORA, jnp.tanh(wa), wa)
    lora = _dot(feed.astype(BF), lora_ref[...])
    w_pre = w0_ref[...] + lora[:, :d_r]
    lw = -jnp.exp(-_softplus(-w_pre) - 0.5)
    a = jax.nn.sigmoid(a0_ref[...] + lora[:, d_r:])
    g = _dot(jax.nn.sigmoid(g_lo).astype(BF), gup_ref[...])
    if has_vres:
        vf = jnp.concatenate([vf_ref[n] for n in range(nb)], axis=0)
        low = _dot(v.astype(BF), vd_ref[...])
        v = v + (vf - v) * jax.nn.sigmoid(vb_ref[...] + _dot(low.astype(BF), vu_ref[...]))
    else:
        for n in range(nb):
            vo_ref[n] = v[n * chunk:(n + 1) * chunk]

    bdm = bdm_ref[...]
    kk = k * kk_ref[...]
    ssq = jnp.concatenate(
        [_dot_lhs2(jnp.square(kk[:, i * LANES:(i + 1) * LANES]), bdm) for i in range(groups)],
        axis=1) * float(HEAD_DIM)
    kk = kk / jnp.maximum(jnp.sqrt(ssq), L2_EPS)
    k = k * (1.0 + (a - 1.0) * ka_ref[...])

    tri = tri_ref[...]
    l1 = lw.astype(BF)
    rem = lw - l1.astype(F32)
    l2 = rem.astype(BF)
    l3 = (rem - l2.astype(F32)).astype(BF)
    cum_parts, total_parts = [], []
    for n in range(nb):
        rows = slice(n * chunk, (n + 1) * chunk)
        c = _dot(tri, l1[rows]) + (_dot(tri, l2[rows]) + _dot(tri, l3[rows]))
        cum_parts.append(c)
        total_parts.append(jnp.broadcast_to(c[chunk - 1:chunk, :], c.shape))
    cum = jnp.concatenate(cum_parts, axis=0)
    total = jnp.concatenate(total_parts, axis=0)
    grow = jnp.exp(-cum)
    to_end = jnp.exp(total - cum)
    b = kk * a
    a_src = -(kk * jnp.exp(cum - lw))
    r_src = r * jnp.exp(cum)
    b_src = b * grow
    k_src = k * grow
    b_end = b * to_end
    k_end = k * to_end
    end_decay = jnp.exp(total)

    row = lax.broadcasted_iota(jnp.int32, (two, LANES), 0)
    col = lax.broadcasted_iota(jnp.int32, (two, LANES), 1)
    same_head = (row < chunk) == (col < HEAD_DIM)
    row_in = row & (chunk - 1)
    col_in = col & (chunk - 1)
    strict = row_in > col_in
    incl = row_in >= col_in
    eye = (row == col).astype(F32)

    chains = [(n, p) for n in range(nb) for p in range(groups)]

    def part(t, n, p):
        return t[n * chunk:(n + 1) * chunk, p * LANES:(p + 1) * LANES]

    def bd(t, n, p):
        s = part(t, n, p)
        return jnp.where(same_head, jnp.concatenate([s, s], axis=0), 0.0)

    a_s = [bd(a_src, n, p).astype(BF) for n, p in chains]
    r_s = [bd(r_src, n, p).astype(BF) for n, p in chains]
    b_s = [bd(b_src, n, p).astype(BF) for n, p in chains]
    k_s = [bd(k_src, n, p).astype(BF) for n, p in chains]
    v_bd = [bd(v, n, p).astype(BF) for n, p in chains]
    st = [st_ref[n, p] for n, p in chains]
    st_b = [s.astype(BF) for s in st]

    pair = [_dot_nt(jnp.concatenate([a_s[c], r_s[c]], axis=0),
                    jnp.concatenate([b_s[c], k_s[c]], axis=0)) for c in range(len(chains))]
    n_ab = [jnp.where(strict, m[:two, :two], 0.0) for m in pair]
    a_ak = [jnp.where(strict, m[:two, two:], 0.0).astype(BF) for m in pair]
    a_rb = [jnp.where(incl, m[two:, :two], 0.0).astype(BF) for m in pair]
    a_rk = [jnp.where(incl, m[two:, two:], 0.0).astype(BF) for m in pair]

    inv = [eye + m for m in n_ab]
    power = [m.astype(BF) for m in n_ab]
    for _ in range(int(math.log2(chunk)) - 1):
        power = [_dot(m, m).astype(BF) for m in power]
        inv = [i + _dot(i.astype(BF), m) for i, m in zip(inv, power)]

    x0 = [_dot(jnp.concatenate([a_s[c], a_ak[c]], axis=1),
               jnp.concatenate([st_b[c], v_bd[c]], axis=0)) for c in range(len(chains))]
    u_b = [_dot(inv[c].astype(BF), x0[c].astype(BF)).astype(BF) for c in range(len(chains))]
    y_bd = [_dot(jnp.concatenate([r_s[c], a_rb[c], a_rk[c]], axis=1),
                 jnp.concatenate([st_b[c], u_b[c], v_bd[c]], axis=0)) for c in range(len(chains))]
    for c, (n, p) in enumerate(chains):
        bk_t = jnp.concatenate([bd(b_end, n, p), bd(k_end, n, p)], axis=0).T.astype(BF)
        decay_col = jnp.concatenate([part(end_decay, n, p)] * 2, axis=0).T
        st_ref[n, p] = decay_col * st[c] + _dot(bk_t, jnp.concatenate([u_b[c], v_bd[c]], axis=0))

    y = jnp.concatenate(
        [jnp.concatenate([y_bd[n * groups + p][:chunk] + y_bd[n * groups + p][chunk:]
                          for p in range(groups)], axis=1) for n in range(nb)], axis=0)
    rkr = r * k * rk_ref[...]
    outs = []
    for p in range(groups):
        sl = slice(p * LANES, (p + 1) * LANES)
        yp = y[:, sl]
        cen = yp - _dot_lhs2(yp, bdm)
        var = _dot_lhs2(cen * cen, bdm)
        yn = cen * lax.rsqrt(var + GN_EPS) * lnw_ref[:, sl] + lnb_ref[:, sl]
        bonus = _dot_lhs2(rkr[:, sl], bdm) * float(HEAD_DIM) * v[:, sl]
        outs.append((yn + bonus) * g[:, sl])
    out = jnp.concatenate(outs, axis=1).astype(o_ref.dtype)
    for n in range(nb):
        o_ref[n] = out[n * chunk:(n + 1) * chunk]


def _rwkv_mix(z, v_first, mu, w0, a0, lora_up, gate_up, k_k, k_a, rk, lnw, lnb, vres, tri, bdm):
    bsz, seq, c_rwkv = z.shape
    d_r = w0.shape[1]
    chunk = RWKV_CHUNK
    nb = RWKV_SEQS
    has_vres = vres is not None
    tok_z = pl.BlockSpec((nb, chunk, c_rwkv), lambda b, c: (b, c, 0))
    tok = pl.BlockSpec((nb, chunk, d_r), lambda b, c: (b, c, 0))
    full = lambda arr: pl.BlockSpec(arr.shape, lambda b, c: (0,) * arr.ndim)
    consts = [mu, w0, a0, lora_up, gate_up, k_k, k_a, rk, lnw, lnb]
    args = [z] + ([v_first] if has_vres else []) + consts + (list(vres) if has_vres else []) + [tri, bdm]
    in_specs = ([tok_z] + ([tok] if has_vres else []) + [full(t) for t in consts]
                + ([full(t) for t in vres] if has_vres else []) + [full(tri), full(bdm)])
    out_specs = [tok] if has_vres else [tok, tok]
    out_shape = [jax.ShapeDtypeStruct((bsz, seq, d_r), BF)]
    if not has_vres:
        out_shape.append(jax.ShapeDtypeStruct((bsz, seq, d_r), F32))
    res = pl.pallas_call(
        functools.partial(_rwkv_kernel, has_vres, d_r),
        grid=(bsz // nb, seq // chunk),
        in_specs=in_specs,
        out_specs=out_specs,
        out_shape=out_shape,
        scratch_shapes=[pltpu.VMEM((nb, d_r // LANES, LANES, LANES), F32),
                        pltpu.VMEM((nb, 8, c_rwkv), F32)],
        compiler_params=_params("arbitrary", "arbitrary"),
        name="rwkv_mix",
    )(*args)
    return (res[0], v_first) if has_vres else (res[0], res[1])


def _sb_kernel(q_ref, k_ref, v_ref, og_ref, tri_ref, bdm_ref, o_ref, acc_ref, carry_ref):
    t = SB_TILE
    qi = pl.program_id(2)
    q = q_ref[0]
    lane = lax.broadcasted_iota(jnp.int32, (t, LANES), 1)
    rowi = lax.broadcasted_iota(jnp.int32, (t, t), 0)
    coli = lax.broadcasted_iota(jnp.int32, (t, t), 1)
    causal = coli < rowi
    zero = jnp.zeros_like(q)
    q_heads = [jnp.where(lane < HEAD_DIM, q, zero), jnp.where(lane >= HEAD_DIM, q, zero)]
    tri = tri_ref[...]

    def scores(j, diagonal):
        start = pl.multiple_of(j * t, t)
        kb = k_ref[0, pl.ds(start, t), :]
        vb = v_ref[0, pl.ds(start, t), :]
        logits = [_dot_nt(q_heads[h], kb) for h in range(2)]
        soft = [_softplus2(s) for s in logits]
        drop = [jnp.where(causal, s, 0.0) for s in soft] if diagonal else soft
        sums = [_dot(d.astype(BF), tri) for d in drop]
        args = [logits[h] - soft[h] - sums[h][:, :t] for h in range(2)]
        return args, [s[:, t:] for s in sums], vb

    def absorb(block, state, diagonal):
        args, totals, vb = block
        out = []
        for h in range(2):
            if diagonal:
                w = jnp.where(causal, jnp.exp2(args[h]), 0.0)
                out.append((_dot(w.astype(BF), vb), totals[h]))
            else:
                acc, carry = state[h]
                w = jnp.exp2(args[h] - jnp.concatenate([carry] * (t // LANES), axis=1))
                out.append((acc + _dot(w.astype(BF), vb), carry + totals[h]))
        return out

    def save(state):
        for h in range(2):
            acc_ref[h], carry_ref[h] = state[h]

    def lightest(state):
        return jnp.minimum(jnp.min(state[0][1]), jnp.min(state[1][1]))

    @pl.when(qi == 0)
    def _():
        save(absorb(scores(qi, True), None, True))

    @pl.when(qi > 0)
    def _():
        newest = scores(qi, True)
        older = scores(qi - 1, False)
        save(absorb(older, absorb(newest, None, True), False))

    def more(state):
        i, mass = state
        return jnp.logical_and(i < qi, mass < SB_DEAD_MASS)

    def body(state):
        i, _ = state
        new = absorb(scores(qi - 1 - i, False), [(acc_ref[h], carry_ref[h]) for h in range(2)], False)
        save(new)
        return i + 1, lightest(new)

    first_mass = jnp.minimum(jnp.min(carry_ref[0]), jnp.min(carry_ref[1]))
    lax.while_loop(more, body, (jnp.int32(1), first_mass))

    o = jnp.where(lane < HEAD_DIM, acc_ref[0], acc_ref[1])
    ms = _dot_lhs2(o * o, bdm_ref[...])
    o_ref[0] = (o * lax.rsqrt(ms + NORM_EPS) * og_ref[...]).astype(o_ref.dtype)


def _sb_attention(q, k, v, og, tri2, bdm):
    bsz, seq, ds = q.shape
    t = SB_TILE
    return pl.pallas_call(
        _sb_kernel,
        grid=(bsz, ds // LANES, seq // t),
        in_specs=[
            pl.BlockSpec((1, t, LANES), lambda b, p, i: (b, i, p)),
            pl.BlockSpec((1, seq, LANES), lambda b, p, i: (b, 0, p)),
            pl.BlockSpec((1, seq, LANES), lambda b, p, i: (b, 0, p)),
            pl.BlockSpec((1, LANES), lambda b, p, i: (0, p)),
            pl.BlockSpec((t, t + LANES), lambda b, p, i: (0, 0)),
            pl.BlockSpec((LANES, LANES), lambda b, p, i: (0, 0)),
        ],
        out_specs=pl.BlockSpec((1, t, LANES), lambda b, p, i: (b, i, p)),
        out_shape=jax.ShapeDtypeStruct((bsz, seq, ds), BF),
        scratch_shapes=[pltpu.VMEM((2, t, LANES), F32), pltpu.VMEM((2, t, LANES), F32)],
        compiler_params=_params("arbitrary", "arbitrary", "arbitrary"),
        name="sb_attention",
    )(q, k, v, og, tri2, bdm)


def _out_proj_kernel(yr_ref, ys_ref, x_ref, mod_ref, g_ref, wr_ref, ws_ref, rwh_ref, rwl_ref, rb_ref,
                     xo_ref, h_ref, idx_ref, gate_ref):
    mixed = _dot(yr_ref[...], wr_ref[...]) + _dot(ys_ref[...], ws_ref[...])
    x = x_ref[...] + mod_ref[0, 2:3, :] * mixed
    xo_ref[...] = x
    ms = jnp.mean(x * x, axis=-1, keepdims=True)
    h = x * lax.rsqrt(ms + NORM_EPS) * g_ref[...]
    h = h * (1.0 + mod_ref[0, 4:5, :]) + mod_ref[0, 3:4, :]
    h_ref[...] = _pack_bf16_pairs(h)

    hh, hl = _split2(h)
    rwh = rwh_ref[...]
    logits = _dot_nt(rwh, hh) + (_dot_nt(rwh, hl) + _dot_nt(rwl_ref[...], hh)) + rb_ref[...]
    eidx = lax.broadcasted_iota(jnp.int32, logits.shape, 0)
    tops, idxs = [], []
    for _ in range(TOP_K):
        m = jnp.max(logits, axis=0, keepdims=True)
        idx = jnp.min(jnp.where(logits == m, eidx, N_EXPERTS), axis=0, keepdims=True)
        tops.append(m)
        idxs.append(idx)
        logits = jnp.where(eidx == idx, -jnp.inf, logits)
    exps = [jnp.exp(m - tops[0]) for m in tops]
    den = exps[0] + exps[1] + exps[2] + exps[3]
    pad = [jnp.zeros_like(den)] * (8 - TOP_K)
    gate_ref[...] = jnp.concatenate([e / den for e in exps] + pad, axis=0)
    idx_ref[...] = jnp.concatenate(idxs + [jnp.zeros_like(idxs[0])] * (8 - TOP_K), axis=0)


def _out_proj(yr, ys, x2, mod, g, wr, ws, rwh, rwl, rb, seq):
    n, d = x2.shape
    dr = yr.shape[1]
    ds = ys.shape[1]
    tm = ROW_TILE
    per_seq = seq // tm
    row = lambda i: (i, 0)
    const = lambda i: (0, 0)
    colblk = lambda i: (0, i)
    return pl.pallas_call(
        _out_proj_kernel,
        grid=(n // tm,),
        in_specs=[
            pl.BlockSpec((tm, dr), row),
            pl.BlockSpec((tm, ds), row),
            pl.BlockSpec((tm, d), row),
            pl.BlockSpec((1,) + mod.shape[1:], lambda i: (i // per_seq, 0, 0)),
            pl.BlockSpec((1, d), const),
            pl.BlockSpec((dr, d), const),
            pl.BlockSpec((ds, d), const),
            pl.BlockSpec((N_EXPERTS, d), const),
            pl.BlockSpec((N_EXPERTS, d), const),
            pl.BlockSpec((N_EXPERTS, 1), const),
        ],
        out_specs=[
            pl.BlockSpec((tm, d), row),
            pl.BlockSpec((tm, d // 2), row),
            pl.BlockSpec((8, tm), colblk),
            pl.BlockSpec((8, tm), colblk),
        ],
        out_shape=[
            jax.ShapeDtypeStruct((n, d), F32),
            jax.ShapeDtypeStruct((n, d // 2), jnp.uint32),
            jax.ShapeDtypeStruct((8, n), jnp.int32),
            jax.ShapeDtypeStruct((8, n), F32),
        ],
        compiler_params=_params("arbitrary"),
        name="out_proj_router",
    )(yr, ys, x2, mod, g, wr, ws, rwh, rwl, rb)


def _moe_kernel(te_ref, nu_ref, x_ref, w1_ref, b1_ref, w2_ref, b2_ref, o_ref, w1b_ref, w2b_ref):
    i = pl.program_id(0)
    f = w2_ref.shape[2]
    live = i < nu_ref[0]
    new_expert = jnp.logical_or(i == 0, te_ref[i] != te_ref[jnp.maximum(i - 1, 0)])

    @pl.when(jnp.logical_and(live, new_expert))
    def _():
        w1b_ref[...] = w1_ref[0, 0].astype(BF)
        w2b_ref[...] = w2_ref[0, 0].astype(BF)

    @pl.when(live)
    def _():
        lo, hi = _unpack_bf16_pairs(x_ref[...])
        x = jnp.concatenate([lo.astype(BF), hi.astype(BF)], axis=1)
        hid = _dot(x, w1b_ref[...]) + b1_ref[0, 0]
        glu = jnp.minimum(hid[:, :f], SWIGLU_LIMIT)
        lin = jnp.clip(hid[:, f:], -SWIGLU_LIMIT, SWIGLU_LIMIT)
        act = glu * jax.nn.sigmoid(SWIGLU_ALPHA * glu) * (lin + 1.0)
        o_ref[...] = _pack_bf16_pairs(_dot(act.astype(BF), w2b_ref[...]) + b2_ref[0, 0])

    @pl.when(jnp.logical_not(live))
    def _():
        o_ref[...] = jnp.zeros_like(o_ref)


def _moe_experts(layer, tile_expert, n_used, xs, w1, b1, w2, b2):
    n_rows, half = xs.shape
    _, _, d, f2 = w1.shape
    f = w2.shape[2]
    tm = MOE_TILE
    expert = lambda i, te, nu: (layer, te[i], 0, 0)
    return pl.pallas_call(
        _moe_kernel,
        grid_spec=pltpu.PrefetchScalarGridSpec(
            num_scalar_prefetch=2,
            grid=(n_rows // tm,),
            in_specs=[
                pl.BlockSpec((tm, half), lambda i, te, nu: (i, 0)),
                pl.BlockSpec((1, 1, d, f2), expert),
                pl.BlockSpec((1, 1, 1, f2), expert),
                pl.BlockSpec((1, 1, f, d), expert),
                pl.BlockSpec((1, 1, 1, d), expert),
            ],
            out_specs=pl.BlockSpec((tm, half), lambda i, te, nu: (i, 0)),
            scratch_shapes=[pltpu.VMEM((d, f2), BF), pltpu.VMEM((f, d), BF)],
        ),
        out_shape=jax.ShapeDtypeStruct((n_rows, half), jnp.uint32),
        compiler_params=pltpu.CompilerParams(dimension_semantics=("arbitrary",),
                                             vmem_limit_bytes=MOE_VMEM_LIMIT),
        name="moe_experts",
    )(tile_expert, n_used, xs, w1, b1.reshape(b1.shape[:2] + (1, f2)), w2,
      b2.reshape(b2.shape[:2] + (1, d)))


def _moe_ffn(layer, h_packed, top_idx, top_gate, w1, b1, w2, b2):
    n_tok = h_packed.shape[0]
    tm = MOE_TILE
    n_assign = n_tok * TOP_K
    n_rows = n_assign + N_EXPERTS * tm
    n_tiles = n_rows // tm
    flat_e = top_idx.reshape(-1)
    order = jnp.argsort(flat_e, stable=True)
    rank_of = jnp.argsort(order)
    counts = jnp.sum((flat_e[:, None] == jnp.arange(N_EXPERTS, dtype=jnp.int32)[None, :]).astype(jnp.int32),
                     axis=0)
    padded = (counts + tm - 1) // tm * tm
    padded_end = jnp.cumsum(padded)
    padded_start = padded_end - padded
    start = jnp.cumsum(counts) - counts
    shift = padded_start - start
    pos = (rank_of + shift[flat_e]).reshape(TOP_K, n_tok)
    tile_start = jnp.arange(n_tiles, dtype=jnp.int32) * tm
    tile_expert = jnp.minimum(
        jnp.sum((tile_start[:, None] >= padded_end[None, :]).astype(jnp.int32), axis=1),
        N_EXPERTS - 1).astype(jnp.int32)
    n_used = (padded_end[-1:] // tm).astype(jnp.int32)
    rows = jnp.arange(n_rows, dtype=jnp.int32)
    row_e = jnp.repeat(tile_expert, tm)
    local = rows - padded_start[row_e]
    src = jnp.clip(start[row_e] + local, 0, n_assign - 1)
    row_tok = jnp.where(local < counts[row_e], order[src] % n_tok, 0)
    ys = _moe_experts(layer, tile_expert, n_used, h_packed[row_tok], w1, b1, w2, b2)
    out = None
    for s in range(TOP_K):
        lo, hi = _unpack_bf16_pairs(ys[pos[s]])
        term = jnp.concatenate([lo, hi], axis=1) * top_gate[s][:, None]
        out = term if out is None else out + term
    return out


def _block_diag_mean():
    i = jnp.arange(LANES)
    return jnp.where((i[:, None] // HEAD_DIM) == (i[None, :] // HEAD_DIM), 1.0 / HEAD_DIM, 0.0).astype(BF)


def _block_diag2(a, b):
    top = jnp.concatenate([a, jnp.zeros((a.shape[0], b.shape[1]), a.dtype)], axis=1)
    bot = jnp.concatenate([jnp.zeros((b.shape[0], a.shape[1]), b.dtype), b], axis=1)
    return jnp.concatenate([top, bot], axis=0)


@jax.jit
def _forward(x, c, norm1_g, norm2_g, ada_w, ada_b, w_in, shift_mu, decay_w0, decay_up, iclr_a0,
             iclr_up, gate_up, k_k, k_a, r_k, lnx_w, lnx_b, vres_down, vres_up, vres_b, q_norm_g,
             k_norm_g, sb_out_g, w_out, router_w, router_b, exp_w1, exp_b1, exp_w2, exp_b2):
    bsz, seq, d = x.shape
    depth = w_in.shape[0]
    d_r = decay_w0.shape[1]
    c_rwkv = shift_mu.shape[1]
    d_sb = sb_out_g.shape[1]
    n_tok = bsz * seq
    assert seq % ROW_TILE == 0 and seq % SB_TILE == 0 and seq % RWKV_CHUNK == 0
    assert bsz % RWKV_SEQS == 0 and DECAY_LORA + ICLR_LORA == LANES and GATE_LORA == LANES

    bdm = _block_diag_mean()
    ci = jnp.arange(RWKV_CHUNK)
    tri_chunk = (ci[None, :] <= ci[:, None]).astype(BF)
    si = jnp.arange(SB_TILE)
    tri_sb = jnp.concatenate([(si[:, None] > si[None, :]).astype(BF),
                              jnp.ones((SB_TILE, LANES), BF)], axis=1)
    row_vec = lambda t: t.reshape(1, -1)

    mod_all = _ada_mod(c, ada_w, ada_b)
    x2 = x.reshape(n_tok, d)
    v_first = None
    for l in range(depth):
        mod = mod_all[l]
        zr, q, k, v = _in_proj(
            x2, mod, row_vec(norm1_g[l]), w_in[l].astype(BF), bdm,
            row_vec(jnp.tile(q_norm_g[l], LANES // HEAD_DIM)),
            row_vec(jnp.tile(k_norm_g[l], LANES // HEAD_DIM)), seq, c_rwkv, d_sb)
        if l == 0:
            vres = None
        else:
            rank = vres_down.shape[2]
            vres = (jnp.pad(vres_down[l - 1], ((0, 0), (0, LANES - rank))).astype(BF),
                    jnp.pad(vres_up[l - 1], ((0, LANES - rank), (0, 0))).astype(BF),
                    row_vec(vres_b[l - 1]))
        y_r, v_first = _rwkv_mix(
            zr.reshape(bsz, seq, c_rwkv), v_first, row_vec(shift_mu[l]), row_vec(decay_w0[l]),
            row_vec(iclr_a0[l]), _block_diag2(decay_up[l], iclr_up[l]).astype(BF),
            gate_up[l].astype(BF), row_vec(k_k[l]), row_vec(k_a[l]), row_vec(r_k[l]),
            row_vec(lnx_w[l]), row_vec(lnx_b[l]), vres, tri_chunk, bdm)
        y_s = _sb_attention(q.reshape(bsz, seq, d_sb), k.reshape(bsz, seq, d_sb),
                            v.reshape(bsz, seq, d_sb), row_vec(sb_out_g[l]), tri_sb, bdm)
        rw_t = router_w[l].T
        rwh = rw_t.astype(BF)
        rwl = (rw_t - rwh.astype(F32)).astype(BF)
        w_o = w_out[l].astype(BF)
        x_mid, h2, top_idx, top_gate = _out_proj(
            y_r.reshape(n_tok, d_r), y_s.reshape(n_tok, d_sb), x2, mod, row_vec(norm2_g[l]),
            w_o[:d_r], w_o[d_r:], rwh, rwl, router_b[l].reshape(N_EXPERTS, 1), seq)
        moe = _moe_ffn(l, h2, top_idx[:TOP_K], top_gate[:TOP_K], exp_w1, exp_b1, exp_w2, exp_b2)
        gate2 = jnp.repeat(mod[:, 5, :], seq, axis=0)
        x2 = x_mid + gate2 * moe
    return x2.reshape(bsz, seq, d)


def kernel(x, c, norm1_g, norm2_g, ada_w, ada_b, w_in, shift_mu, decay_w0, decay_up, iclr_a0, iclr_up,
           gate_up, k_k, k_a, r_k, lnx_w, lnx_b, vres_down, vres_up, vres_b, q_norm_g, k_norm_g,
           sb_out_g, w_out, router_w, router_b, exp_w1, exp_b1, exp_w2, exp_b2):
    return _forward(x, c, norm1_g, norm2_g, ada_w, ada_b, w_in, shift_mu, decay_w0, decay_up, iclr_a0,
                    iclr_up, gate_up, k_k, k_a, r_k, lnx_w, lnx_b, vres_down, vres_up, vres_b,
                    q_norm_g, k_norm_g, sb_out_g, w_out, router_w, router_b, exp_w1, exp_b1,
                    exp_w2, exp_b2)
```

```python
import functools
import math

import jax
import jax.numpy as jnp
from jax import lax
from jax.experimental import pallas as pl
from jax.experimental.pallas import tpu as pltpu

F32 = jnp.float32
BF = jnp.bfloat16

HEAD_DIM = 64
LANES = 128
RWKV_CHUNK = 64
RWKV_SEQS = 2
DECAY_LORA = 64
ICLR_LORA = 64
GATE_LORA = 128
N_EXPERTS = 32
TOP_K = 4
SWIGLU_ALPHA = 1.702
SWIGLU_LIMIT = 7.0
NORM_EPS = 1e-6
GN_EPS = 1e-5 * HEAD_DIM
L2_EPS = 1e-12

ROW_TILE = 256
SB_TILE = 256
MOE_TILE = 512
VMEM_LIMIT = 48 * 1024 * 1024
MOE_VMEM_LIMIT = 58 * 1024 * 1024
LOG2E = 1.4426950408889634
SB_DEAD_MASS = 150.0


def _pack_bf16_pairs(x):
    n = x.shape[1] // 2
    lo = lax.bitcast_convert_type(x[:, :n].astype(BF).astype(F32), jnp.uint32)
    hi = lax.bitcast_convert_type(x[:, n:].astype(BF).astype(F32), jnp.uint32)
    return hi | (lo >> 16)


def _unpack_bf16_pairs(w):
    lo = lax.bitcast_convert_type(w << 16, F32)
    hi = lax.bitcast_convert_type(w & jnp.uint32(0xFFFF0000), F32)
    return lo, hi


def _dot(a, b):
    return jnp.dot(a, b, preferred_element_type=F32)


def _dot_nt(a, b):
    return lax.dot_general(a, b, (((1,), (1,)), ((), ())), preferred_element_type=F32)


def _split2(x):
    hi = x.astype(BF)
    lo = (x - hi.astype(F32)).astype(BF)
    return hi, lo


def _dot_lhs2(x, m_bf):
    hi, lo = _split2(x)
    return _dot(hi, m_bf) + _dot(lo, m_bf)


def _dot3(a, b):
    ah, al = _split2(a)
    bh, bl = _split2(b)
    return _dot(ah, bh) + (_dot(ah, bl) + _dot(al, bh))


def _softplus(x):
    return jnp.maximum(x, 0.0) + jnp.log(1.0 + jnp.exp(-jnp.abs(x)))


def _softplus2(x):
    return jnp.maximum(x, 0.0) + jnp.log2(1.0 + jnp.exp2(-jnp.abs(x)))


def _params(*sem):
    return pltpu.CompilerParams(dimension_semantics=sem, vmem_limit_bytes=VMEM_LIMIT)


def _ada_kernel(c_ref, w_ref, b_ref, o_ref):
    c = c_ref[...]
    cond = c * jax.nn.sigmoid(c)
    o_ref[0] = _dot3(cond, w_ref[0]) + b_ref[0]


def _ada_mod(c, ada_w, ada_b):
    depth, d, d6 = ada_w.shape
    bsz = c.shape[0]
    out = pl.pallas_call(
        _ada_kernel,
        grid=(depth, d6 // d),
        in_specs=[
            pl.BlockSpec((bsz, d), lambda l, j: (0, 0)),
            pl.BlockSpec((1, d, d), lambda l, j: (l, 0, j)),
            pl.BlockSpec((1, 1, d), lambda l, j: (l, 0, j)),
        ],
        out_specs=pl.BlockSpec((1, bsz, d), lambda l, j: (l, 0, j)),
        out_shape=jax.ShapeDtypeStruct((depth, bsz, d6), F32),
        compiler_params=_params("arbitrary", "arbitrary"),
        name="ada_mod",
    )(c, ada_w, ada_b.reshape(depth, 1, d6))
    return out.reshape(depth, bsz, d6 // d, d)


def _in_proj_kernel(c_rwkv, d_sb, x_ref, mod_ref, g_ref, w_ref, bdm_ref, qg_ref, kg_ref,
                    zr_ref, q_ref, k_ref, v_ref):
    x = x_ref[...]
    ms = jnp.mean(x * x, axis=-1, keepdims=True)
    h = x * lax.rsqrt(ms + NORM_EPS) * g_ref[...]
    h = h * (1.0 + mod_ref[0, 1:2, :]) + mod_ref[0, 0:1, :]
    z = _dot(h.astype(BF), w_ref[...])
    zr_ref[...] = z[:, :c_rwkv]
    bdm = bdm_ref[...]
    scale = LOG2E / math.sqrt(HEAD_DIM)
    for p in range(d_sb // LANES):
        lo = p * LANES
        q = z[:, c_rwkv + lo:c_rwkv + lo + LANES]
        k = z[:, c_rwkv + d_sb + lo:c_rwkv + d_sb + lo + LANES]
        qn = q * lax.rsqrt(_dot_lhs2(q * q, bdm) + NORM_EPS) * qg_ref[...]
        kn = k * lax.rsqrt(_dot_lhs2(k * k, bdm) + NORM_EPS) * kg_ref[...]
        q_ref[:, lo:lo + LANES] = (qn * scale).astype(BF)
        k_ref[:, lo:lo + LANES] = kn.astype(BF)
    v_ref[...] = z[:, c_rwkv + 2 * d_sb:].astype(BF)


def _in_proj(x2, mod, g, w_bf, bdm, qg, kg, seq, c_rwkv, d_sb):
    n, d = x2.shape
    c_in = w_bf.shape[1]
    tm = ROW_TILE
    per_seq = seq // tm
    row = lambda i: (i, 0)
    const = lambda i: (0, 0)
    return pl.pallas_call(
        functools.partial(_in_proj_kernel, c_rwkv, d_sb),
        grid=(n // tm,),
        in_specs=[
            pl.BlockSpec((tm, d), row),
            pl.BlockSpec((1,) + mod.shape[1:], lambda i: (i // per_seq, 0, 0)),
            pl.BlockSpec((1, d), const),
            pl.BlockSpec((d, c_in), const),
            pl.BlockSpec((LANES, LANES), const),
            pl.BlockSpec((1, LANES), const),
            pl.BlockSpec((1, LANES), const),
        ],
        out_specs=[
            pl.BlockSpec((tm, c_rwkv), row),
            pl.BlockSpec((tm, d_sb), row),
            pl.BlockSpec((tm, d_sb), row),
            pl.BlockSpec((tm, d_sb), row),
        ],
        out_shape=[
            jax.ShapeDtypeStruct((n, c_rwkv), F32),
            jax.ShapeDtypeStruct((n, d_sb), BF),
            jax.ShapeDtypeStruct((n, d_sb), BF),
            jax.ShapeDtypeStruct((n, d_sb), BF),
        ],
        compiler_params=_params("arbitrary"),
        name="in_proj",
    )(x2, mod, g, w_bf, bdm, qg, kg)


def _rwkv_kernel(has_vres, d_r, *refs):
    if has_vres:
        (z_ref, vf_ref, mu_ref, w0_ref, a0_ref, lora_ref, gup_ref, kk_ref, ka_ref, rk_ref, lnw_ref,
         lnb_ref, vd_ref, vu_ref, vb_ref, tri_ref, bdm_ref, o_ref, st_ref, prev_ref) = refs
    else:
        (z_ref, mu_ref, w0_ref, a0_ref, lora_ref, gup_ref, kk_ref, ka_ref, rk_ref, lnw_ref,
         lnb_ref, tri_ref, bdm_ref, o_ref, vo_ref, st_ref, prev_ref) = refs
    chunk = RWKV_CHUNK
    two = 2 * chunk
    nb = z_ref.shape[0]
    groups = d_r // LANES

    @pl.when(pl.program_id(1) == 0)
    def _():
        st_ref[...] = jnp.zeros_like(st_ref)
        prev_ref[...] = jnp.zeros_like(prev_ref)

    first = lax.broadcasted_iota(jnp.int32, (chunk, z_ref.shape[2]), 0) == 0
    mu = mu_ref[...]
    zs_parts = []
    for n in range(nb):
        z = z_ref[n]
        z_prev = jnp.where(first, prev_ref[n, 0:1, :], pltpu.roll(z, 1, axis=0))
        prev_ref[n, 0:1, :] = z[chunk - 1:chunk, :]
        zs_parts.append(z + (z_prev - z) * mu)
    zs = jnp.concatenate(zs_parts, axis=0)

    r = zs[:, :d_r]
    k = zs[:, d_r:2 * d_r]
    v = zs[:, 2 * d_r:3 * d_r]
    wa = zs[:, 3 * d_r:3 * d_r + LANES]
    g_lo = zs[:, 3 * d_r + LANES:]
    lane = lax.broadcasted_iota(jnp.int32, wa.shape, 1)
    feed = jnp.where(lane < DECAY_LORA, jnp.tanh(wa), wa)
    lora = _dot(feed.astype(BF), lora_ref[...])
    w_pre = w0_ref[...] + lora[:, :d_r]
    lw = -jnp.exp(-_softplus(-w_pre) - 0.5)
    a = jax.nn.sigmoid(a0_ref[...] + lora[:, d_r:])
    g = _dot(jax.nn.sigmoid(g_lo).astype(BF), gup_ref[...])
    if has_vres:
        vf = jnp.concatenate([vf_ref[n] for n in range(nb)], axis=0)
        low = _dot(v.astype(BF), vd_ref[...])
        v = v + (vf - v) * jax.nn.sigmoid(vb_ref[...] + _dot(low.astype(BF), vu_ref[...]))
    else:
        for n in range(nb):
            vo_ref[n] = v[n * chunk:(n + 1) * chunk]

    bdm = bdm_ref[...]
    kk = k * kk_ref[...]
    ssq = jnp.concatenate(
        [_dot_lhs2(jnp.square(kk[:, i * LANES:(i + 1) * LANES]), bdm) for i in range(groups)],
        axis=1) * float(HEAD_DIM)
    kk = kk / jnp.maximum(jnp.sqrt(ssq), L2_EPS)
    k = k * (1.0 + (a - 1.0) * ka_ref[...])

    tri = tri_ref[...]
    l1 = lw.astype(BF)
    rem = lw - l1.astype(F32)
    l2 = rem.astype(BF)
    l3 = (rem - l2.astype(F32)).astype(BF)
    cum_parts, total_parts = [], []
    for n in range(nb):
        rows = slice(n * chunk, (n + 1) * chunk)
        c = _dot(tri, l1[rows]) + (_dot(tri, l2[rows]) + _dot(tri, l3[rows]))
        cum_parts.append(c)
        total_parts.append(jnp.broadcast_to(c[chunk - 1:chunk, :], c.shape))
    cum = jnp.concatenate(cum_parts, axis=0)
    total = jnp.concatenate(total_parts, axis=0)
    grow = jnp.exp(-cum)
    to_end = jnp.exp(total - cum)
    b = kk * a
    a_src = -(kk * jnp.exp(cum - lw))
    r_src = r * jnp.exp(cum)
    b_src = b * grow
    k_src = k * grow
    b_end = b * to_end
    k_end = k * to_end
    end_decay = jnp.exp(total)

    row = lax.broadcasted_iota(jnp.int32, (two, LANES), 0)
    col = lax.broadcasted_iota(jnp.int32, (two, LANES), 1)
    same_head = (row < chunk) == (col < HEAD_DIM)
    row_in = row & (chunk - 1)
    col_in = col & (chunk - 1)
    strict = row_in > col_in
    incl = row_in >= col_in
    eye = (row == col).astype(F32)

    chains = [(n, p) for n in range(nb) for p in range(groups)]

    def part(t, n, p):
        return t[n * chunk:(n + 1) * chunk, p * LANES:(p + 1) * LANES]

    def bd(t, n, p):
        s = part(t, n, p)
        return jnp.where(same_head, jnp.concatenate([s, s], axis=0), 0.0)

    a_s = [bd(a_src, n, p).astype(BF) for n, p in chains]
    r_s = [bd(r_src, n, p).astype(BF) for n, p in chains]
    b_s = [bd(b_src, n, p).astype(BF) for n, p in chains]
    k_s = [bd(k_src, n, p).astype(BF) for n, p in chains]
    v_bd = [bd(v, n, p).astype(BF) for n, p in chains]
    st = [st_ref[n, p] for n, p in chains]
    st_b = [s.astype(BF) for s in st]

    pair = [_dot_nt(jnp.concatenate([a_s[c], r_s[c]], axis=0),
                    jnp.concatenate([b_s[c], k_s[c]], axis=0)) for c in range(len(chains))]
    n_ab = [jnp.where(strict, m[:two, :two], 0.0) for m in pair]
    a_ak = [jnp.where(strict, m[:two, two:], 0.0).astype(BF) for m in pair]
    a_rb = [jnp.where(incl, m[two:, :two], 0.0).astype(BF) for m in pair]
    a_rk = [jnp.where(incl, m[two:, two:], 0.0).astype(BF) for m in pair]

    inv = [eye + m for m in n_ab]
    power = [m.astype(BF) for m in n_ab]
    for _ in range(int(math.log2(chunk)) - 1):
        power = [_dot(m, m).astype(BF) for m in power]
        inv = [i + _dot(i.astype(BF), m) for i, m in zip(inv, power)]

    x0 = [_dot(jnp.concatenate([a_s[c], a_ak[c]], axis=1),
               jnp.concatenate([st_b[c], v_bd[c]], axis=0)) for c in range(len(chains))]
    u_b = [_dot(inv[c].astype(BF), x0[c].astype(BF)).astype(BF) for c in range(len(chains))]
    y_bd = [_dot(jnp.concatenate([r_s[c], a_rb[c], a_rk[c]], axis=1),
                 jnp.concatenate([st_b[c], u_b[c], v_bd[c]], axis=0)) for c in range(len(chains))]
    for c, (n, p) in enumerate(chains):
        bk_t = jnp.concatenate([bd(b_end, n, p), bd(k_end, n, p)], axis=0).T.astype(BF)
        decay_col = jnp.concatenate([part(end_decay, n, p)] * 2, axis=0).T
        st_ref[n, p] = decay_col * st[c] + _dot(bk_t, jnp.concatenate([u_b[c], v_bd[c]], axis=0))

    y = jnp.concatenate(
        [jnp.concatenate([y_bd[n * groups + p][:chunk] + y_bd[n * groups + p][chunk:]
                          for p in range(groups)], axis=1) for n in range(nb)], axis=0)
    rkr = r * k * rk_ref[...]
    outs = []
    for p in range(groups):
        sl = slice(p * LANES, (p + 1) * LANES)
        yp = y[:, sl]
        cen = yp - _dot_lhs2(yp, bdm)
        var = _dot_lhs2(cen * cen, bdm)
        yn = cen * lax.rsqrt(var + GN_EPS) * lnw_ref[:, sl] + lnb_ref[:, sl]
        bonus = _dot_lhs2(rkr[:, sl], bdm) * float(HEAD_DIM) * v[:, sl]
        outs.append((yn + bonus) * g[:, sl])
    out = jnp.concatenate(outs, axis=1).astype(o_ref.dtype)
    for n in range(nb):
        o_ref[n] = out[n * chunk:(n + 1) * chunk]


def _rwkv_mix(z, v_first, mu, w0, a0, lora_up, gate_up, k_k, k_a, rk, lnw, lnb, vres, tri, bdm):
    bsz, seq, c_rwkv = z.shape
    d_r = w0.shape[1]
    chunk = RWKV_CHUNK
    nb = RWKV_SEQS
    has_vres = vres is not None
    tok_z = pl.BlockSpec((nb, chunk, c_rwkv), lambda b, c: (b, c, 0))
    tok = pl.BlockSpec((nb, chunk, d_r), lambda b, c: (b, c, 0))
    full = lambda arr: pl.BlockSpec(arr.shape, lambda b, c: (0,) * arr.ndim)
    consts = [mu, w0, a0, lora_up, gate_up, k_k, k_a, rk, lnw, lnb]
    args = [z] + ([v_first] if has_vres else []) + consts + (list(vres) if has_vres else []) + [tri, bdm]
    in_specs = ([tok_z] + ([tok] if has_vres else []) + [full(t) for t in consts]
                + ([full(t) for t in vres] if has_vres else []) + [full(tri), full(bdm)])
    out_specs = [tok] if has_vres else [tok, tok]
    out_shape = [jax.ShapeDtypeStruct((bsz, seq, d_r), BF)]
    if not has_vres:
        out_shape.append(jax.ShapeDtypeStruct((bsz, seq, d_r), F32))
    res = pl.pallas_call(
        functools.partial(_rwkv_kernel, has_vres, d_r),
        grid=(bsz // nb, seq // chunk),
        in_specs=in_specs,
        out_specs=out_specs,
        out_shape=out_shape,
        scratch_shapes=[pltpu.VMEM((nb, d_r // LANES, LANES, LANES), F32),
                        pltpu.VMEM((nb, 8, c_rwkv), F32)],
        compiler_params=_params("arbitrary", "arbitrary"),
        name="rwkv_mix",
    )(*args)
    return (res[0], v_first) if has_vres else (res[0], res[1])


def _sb_kernel(q_ref, k_ref, v_ref, og_ref, tri_ref, bdm_ref, o_ref, acc_ref, carry_ref):
    t = SB_TILE
    qi = pl.program_id(2)
    q = q_ref[0]
    lane = lax.broadcasted_iota(jnp.int32, (t, LANES), 1)
    rowi = lax.broadcasted_iota(jnp.int32, (t, t), 0)
    coli = lax.broadcasted_iota(jnp.int32, (t, t), 1)
    causal = coli < rowi
    zero = jnp.zeros_like(q)
    q_heads = [jnp.where(lane < HEAD_DIM, q, zero), jnp.where(lane >= HEAD_DIM, q, zero)]
    tri = tri_ref[...]

    def scores(j, diagonal):
        start = pl.multiple_of(j * t, t)
        kb = k_ref[0, pl.ds(start, t), :]
        vb = v_ref[0, pl.ds(start, t), :]
        logits = [_dot_nt(q_heads[h], kb) for h in range(2)]
        soft = [_softplus2(s) for s in logits]
        drop = [jnp.where(causal, s, 0.0) for s in soft] if diagonal else soft
        sums = [_dot(d.astype(BF), tri) for d in drop]
        args = [logits[h] - soft[h] - sums[h][:, :t] for h in range(2)]
        return args, [s[:, t:] for s in sums], vb

    def absorb(block, state, diagonal):
        args, totals, vb = block
        out = []
        for h in range(2):
            if diagonal:
                w = jnp.where(causal, jnp.exp2(args[h]), 0.0)
                out.append((_dot(w.astype(BF), vb), totals[h]))
            else:
                acc, carry = state[h]
                w = jnp.exp2(args[h] - jnp.concatenate([carry] * (t // LANES), axis=1))
                out.append((acc + _dot(w.astype(BF), vb), carry + totals[h]))
        return out

    def save(state):
        for h in range(2):
            acc_ref[h], carry_ref[h] = state[h]

    def lightest(state):
        return jnp.minimum(jnp.min(state[0][1]), jnp.min(state[1][1]))

    @pl.when(qi == 0)
    def _():
        save(absorb(scores(qi, True), None, True))

    @pl.when(qi > 0)
    def _():
        newest = scores(qi, True)
        older = scores(qi - 1, False)
        save(absorb(older, absorb(newest, None, True), False))

    def more(state):
        i, mass = state
        return jnp.logical_and(i < qi, mass < SB_DEAD_MASS)

    def body(state):
        i, _ = state
        new = absorb(scores(qi - 1 - i, False), [(acc_ref[h], carry_ref[h]) for h in range(2)], False)
        save(new)
        return i + 1, lightest(new)

    first_mass = jnp.minimum(jnp.min(carry_ref[0]), jnp.min(carry_ref[1]))
    lax.while_loop(more, body, (jnp.int32(1), first_mass))

    o = jnp.where(lane < HEAD_DIM, acc_ref[0], acc_ref[1])
    ms = _dot_lhs2(o * o, bdm_ref[...])
    o_ref[0] = (o * lax.rsqrt(ms + NORM_EPS) * og_ref[...]).astype(o_ref.dtype)


def _sb_attention(q, k, v, og, tri2, bdm):
    bsz, seq, ds = q.shape
    t = SB_TILE
    return pl.pallas_call(
        _sb_kernel,
        grid=(bsz, ds // LANES, seq // t),
        in_specs=[
            pl.BlockSpec((1, t, LANES), lambda b, p, i: (b, i, p)),
            pl.BlockSpec((1, seq, LANES), lambda b, p, i: (b, 0, p)),
            pl.BlockSpec((1, seq, LANES), lambda b, p, i: (b, 0, p)),
            pl.BlockSpec((1, LANES), lambda b, p, i: (0, p)),
            pl.BlockSpec((t, t + LANES), lambda b, p, i: (0, 0)),
            pl.BlockSpec((LANES, LANES), lambda b, p, i: (0, 0)),
        ],
        out_specs=pl.BlockSpec((1, t, LANES), lambda b, p, i: (b, i, p)),
        out_shape=jax.ShapeDtypeStruct((bsz, seq, ds), BF),
        scratch_shapes=[pltpu.VMEM((2, t, LANES), F32), pltpu.VMEM((2, t, LANES), F32)],
        compiler_params=_params("arbitrary", "arbitrary", "arbitrary"),
        name="sb_attention",
    )(q, k, v, og, tri2, bdm)


def _out_proj_kernel(yr_ref, ys_ref, x_ref, mod_ref, g_ref, wr_ref, ws_ref, rwh_ref, rwl_ref, rb_ref,
                     xo_ref, h_ref, idx_ref, gate_ref):
    mixed = _dot(yr_ref[...], wr_ref[...]) + _dot(ys_ref[...], ws_ref[...])
    x = x_ref[...] + mod_ref[0, 2:3, :] * mixed
    xo_ref[...] = x
    ms = jnp.mean(x * x, axis=-1, keepdims=True)
    h = x * lax.rsqrt(ms + NORM_EPS) * g_ref[...]
    h = h * (1.0 + mod_ref[0, 4:5, :]) + mod_ref[0, 3:4, :]
    h_ref[...] = _pack_bf16_pairs(h)

    hh, hl = _split2(h)
    rwh = rwh_ref[...]
    logits = _dot_nt(rwh, hh) + (_dot_nt(rwh, hl) + _dot_nt(rwl_ref[...], hh)) + rb_ref[...]
    eidx = lax.broadcasted_iota(jnp.int32, logits.shape, 0)
    tops, idxs = [], []
    for _ in range(TOP_K):
        m = jnp.max(logits, axis=0, keepdims=True)
        idx = jnp.min(jnp.where(logits == m, eidx, N_EXPERTS), axis=0, keepdims=True)
        tops.append(m)
        idxs.append(idx)
        logits = jnp.where(eidx == idx, -jnp.inf, logits)
    exps = [jnp.exp(m - tops[0]) for m in tops]
    den = exps[0] + exps[1] + exps[2] + exps[3]
    pad = [jnp.zeros_like(den)] * (8 - TOP_K)
    gate_ref[...] = jnp.concatenate([e / den for e in exps] + pad, axis=0)
    idx_ref[...] = jnp.concatenate(idxs + [jnp.zeros_like(idxs[0])] * (8 - TOP_K), axis=0)


def _out_proj(yr, ys, x2, mod, g, wr, ws, rwh, rwl, rb, seq):
    n, d = x2.shape
    dr = yr.shape[1]
    ds = ys.shape[1]
    tm = ROW_TILE
    per_seq = seq // tm
    row = lambda i: (i, 0)
    const = lambda i: (0, 0)
    colblk = lambda i: (0, i)
    return pl.pallas_call(
        _out_proj_kernel,
        grid=(n // tm,),
        in_specs=[
            pl.BlockSpec((tm, dr), row),
            pl.BlockSpec((tm, ds), row),
            pl.BlockSpec((tm, d), row),
            pl.BlockSpec((1,) + mod.shape[1:], lambda i: (i // per_seq, 0, 0)),
            pl.BlockSpec((1, d), const),
            pl.BlockSpec((dr, d), const),
            pl.BlockSpec((ds, d), const),
            pl.BlockSpec((N_EXPERTS, d), const),
            pl.BlockSpec((N_EXPERTS, d), const),
            pl.BlockSpec((N_EXPERTS, 1), const),
        ],
        out_specs=[
            pl.BlockSpec((tm, d), row),
            pl.BlockSpec((tm, d // 2), row),
            pl.BlockSpec((8, tm), colblk),
            pl.BlockSpec((8, tm), colblk),
        ],
        out_shape=[
            jax.ShapeDtypeStruct((n, d), F32),
            jax.ShapeDtypeStruct((n, d // 2), jnp.uint32),
            jax.ShapeDtypeStruct((8, n), jnp.int32),
            jax.ShapeDtypeStruct((8, n), F32),
        ],
        compiler_params=_params("arbitrary"),
        name="out_proj_router",
    )(yr, ys, x2, mod, g, wr, ws, rwh, rwl, rb)


def _moe_kernel(te_ref, nu_ref, x_ref, gate_ref, w1_ref, b1_ref, w2_ref, b2_ref, o_ref, w1b_ref, w2b_ref):
    i = pl.program_id(0)
    f = w2_ref.shape[2]
    live = i < nu_ref[0]
    new_expert = jnp.logical_or(i == 0, te_ref[i] != te_ref[jnp.maximum(i - 1, 0)])

    @pl.when(jnp.logical_and(live, new_expert))
    def _():
        w1b_ref[...] = w1_ref[0, 0].astype(BF)
        w2b_ref[...] = w2_ref[0, 0].astype(BF)

    @pl.when(live)
    def _():
        lo, hi = _unpack_bf16_pairs(x_ref[...])
        x = jnp.concatenate([lo.astype(BF), hi.astype(BF)], axis=1)
        hid = _dot(x, w1b_ref[...]) + b1_ref[0, 0]
        glu = jnp.minimum(hid[:, :f], SWIGLU_LIMIT)
        lin = jnp.clip(hid[:, f:], -SWIGLU_LIMIT, SWIGLU_LIMIT)
        act = glu * jax.nn.sigmoid(SWIGLU_ALPHA * glu) * (lin + 1.0)
        y = _dot(act.astype(BF), w2b_ref[...]) + b2_ref[0, 0]
        o_ref[...] = _pack_bf16_pairs(y * gate_ref[...])

    @pl.when(jnp.logical_not(live))
    def _():
        o_ref[...] = jnp.zeros_like(o_ref)


def _moe_experts(layer, tile_expert, n_used, xs, row_gate, w1, b1, w2, b2):
    n_rows, half = xs.shape
    _, _, d, f2 = w1.shape
    f = w2.shape[2]
    tm = MOE_TILE
    expert = lambda i, te, nu: (layer, te[i], 0, 0)
    return pl.pallas_call(
        _moe_kernel,
        grid_spec=pltpu.PrefetchScalarGridSpec(
            num_scalar_prefetch=2,
            grid=(n_rows // tm,),
            in_specs=[
                pl.BlockSpec((tm, half), lambda i, te, nu: (i, 0)),
                pl.BlockSpec((tm, 1), lambda i, te, nu: (i, 0)),
                pl.BlockSpec((1, 1, d, f2), expert),
                pl.BlockSpec((1, 1, 1, f2), expert),
                pl.BlockSpec((1, 1, f, d), expert),
                pl.BlockSpec((1, 1, 1, d), expert),
            ],
            out_specs=pl.BlockSpec((tm, half), lambda i, te, nu: (i, 0)),
            scratch_shapes=[pltpu.VMEM((d, f2), BF), pltpu.VMEM((f, d), BF)],
        ),
        out_shape=jax.ShapeDtypeStruct((n_rows, half), jnp.uint32),
        compiler_params=pltpu.CompilerParams(dimension_semantics=("arbitrary",),
                                             vmem_limit_bytes=MOE_VMEM_LIMIT),
        name="moe_experts",
    )(tile_expert, n_used, xs, row_gate, w1, b1.reshape(b1.shape[:2] + (1, f2)), w2,
      b2.reshape(b2.shape[:2] + (1, d)))


def _combine_kernel(x_ref, mod_ref, y0_ref, y1_ref, y2_ref, y3_ref, o_ref):
    half = y0_ref.shape[1]
    lo, hi = _unpack_bf16_pairs(y0_ref[...])
    for y_ref in (y1_ref, y2_ref, y3_ref):
        l2, h2 = _unpack_bf16_pairs(y_ref[...])
        lo = lo + l2
        hi = hi + h2
    o_ref[:, :half] = x_ref[:, :half] + mod_ref[0, 5:6, :half] * lo
    o_ref[:, half:] = x_ref[:, half:] + mod_ref[0, 5:6, half:] * hi


def _combine(x_mid, mod, ys_by_slot, seq):
    n, d = x_mid.shape
    tm = ROW_TILE
    per_seq = seq // tm
    row = lambda i: (i, 0)
    return pl.pallas_call(
        _combine_kernel,
        grid=(n // tm,),
        in_specs=[pl.BlockSpec((tm, d), row),
                  pl.BlockSpec((1,) + mod.shape[1:], lambda i: (i // per_seq, 0, 0))]
                 + [pl.BlockSpec((tm, d // 2), row)] * TOP_K,
        out_specs=pl.BlockSpec((tm, d), row),
        out_shape=jax.ShapeDtypeStruct((n, d), F32),
        compiler_params=_params("arbitrary"),
        name="moe_combine",
    )(x_mid, mod, *ys_by_slot)


def _moe_ffn(layer, h_packed, top_idx, top_gate, w1, b1, w2, b2):
    n_tok = h_packed.shape[0]
    tm = MOE_TILE
    n_assign = n_tok * TOP_K
    n_rows = n_assign + N_EXPERTS * tm
    n_tiles = n_rows // tm
    flat_e = top_idx.reshape(-1)
    order = jnp.argsort(flat_e, stable=True)
    rank_of = jnp.argsort(order)
    counts = jnp.sum((flat_e[:, None] == jnp.arange(N_EXPERTS, dtype=jnp.int32)[None, :]).astype(jnp.int32),
                     axis=0)
    padded = (counts + tm - 1) // tm * tm
    padded_end = jnp.cumsum(padded)
    padded_start = padded_end - padded
    start = jnp.cumsum(counts) - counts
    shift = padded_start - start
    pos = (rank_of + shift[flat_e]).reshape(TOP_K, n_tok)
    tile_start = jnp.arange(n_tiles, dtype=jnp.int32) * tm
    tile_expert = jnp.minimum(
        jnp.sum((tile_start[:, None] >= padded_end[None, :]).astype(jnp.int32), axis=1),
        N_EXPERTS - 1).astype(jnp.int32)
    n_used = (padded_end[-1:] // tm).astype(jnp.int32)
    rows = jnp.arange(n_rows, dtype=jnp.int32)
    row_e = jnp.repeat(tile_expert, tm)
    local = rows - padded_start[row_e]
    src = jnp.clip(start[row_e] + local, 0, n_assign - 1)
    valid = local < counts[row_e]
    assign = order[src]
    row_tok = jnp.where(valid, assign % n_tok, rows % n_tok)
    row_gate = jnp.where(valid, top_gate.reshape(-1)[assign], 0.0).reshape(n_rows, 1)
    ys = _moe_experts(layer, tile_expert, n_used, h_packed[row_tok], row_gate, w1, b1, w2, b2)
    return [ys[pos[s]] for s in range(TOP_K)]


def _block_diag_mean():
    i = jnp.arange(LANES)
    return jnp.where((i[:, None] // HEAD_DIM) == (i[None, :] // HEAD_DIM), 1.0 / HEAD_DIM, 0.0).astype(BF)


def _block_diag2(a, b):
    top = jnp.concatenate([a, jnp.zeros((a.shape[0], b.shape[1]), a.dtype)], axis=1)
    bot = jnp.concatenate([jnp.zeros((b.shape[0], a.shape[1]), b.dtype), b], axis=1)
    return jnp.concatenate([top, bot], axis=0)


@jax.jit
def _forward(x, c, norm1_g, norm2_g, ada_w, ada_b, w_in, shift_mu, decay_w0, decay_up, iclr_a0,
             iclr_up, gate_up, k_k, k_a, r_k, lnx_w, lnx_b, vres_down, vres_up, vres_b, q_norm_g,
             k_norm_g, sb_out_g, w_out, router_w, router_b, exp_w1, exp_b1, exp_w2, exp_b2):
    bsz, seq, d = x.shape
    depth = w_in.shape[0]
    d_r = decay_w0.shape[1]
    c_rwkv = shift_mu.shape[1]
    d_sb = sb_out_g.shape[1]
    n_tok = bsz * seq
    assert seq % ROW_TILE == 0 and seq % SB_TILE == 0 and seq % RWKV_CHUNK == 0
    assert bsz % RWKV_SEQS == 0 and DECAY_LORA + ICLR_LORA == LANES and GATE_LORA == LANES

    bdm = _block_diag_mean()
    ci = jnp.arange(RWKV_CHUNK)
    tri_chunk = (ci[None, :] <= ci[:, None]).astype(BF)
    si = jnp.arange(SB_TILE)
    tri_sb = jnp.concatenate([(si[:, None] > si[None, :]).astype(BF),
                              jnp.ones((SB_TILE, LANES), BF)], axis=1)
    row_vec = lambda t: t.reshape(1, -1)

    mod_all = _ada_mod(c, ada_w, ada_b)
    x2 = x.reshape(n_tok, d)
    v_first = None
    for l in range(depth):
        mod = mod_all[l]
        zr, q, k, v = _in_proj(
            x2, mod, row_vec(norm1_g[l]), w_in[l].astype(BF), bdm,
            row_vec(jnp.tile(q_norm_g[l], LANES // HEAD_DIM)),
            row_vec(jnp.tile(k_norm_g[l], LANES // HEAD_DIM)), seq, c_rwkv, d_sb)
        if l == 0:
            vres = None
        else:
            rank = vres_down.shape[2]
            vres = (jnp.pad(vres_down[l - 1], ((0, 0), (0, LANES - rank))).astype(BF),
                    jnp.pad(vres_up[l - 1], ((0, LANES - rank), (0, 0))).astype(BF),
                    row_vec(vres_b[l - 1]))
        y_r, v_first = _rwkv_mix(
            zr.reshape(bsz, seq, c_rwkv), v_first, row_vec(shift_mu[l]), row_vec(decay_w0[l]),
            row_vec(iclr_a0[l]), _block_diag2(decay_up[l], iclr_up[l]).astype(BF),
            gate_up[l].astype(BF), row_vec(k_k[l]), row_vec(k_a[l]), row_vec(r_k[l]),
            row_vec(lnx_w[l]), row_vec(lnx_b[l]), vres, tri_chunk, bdm)
        y_s = _sb_attention(q.reshape(bsz, seq, d_sb), k.reshape(bsz, seq, d_sb),
                            v.reshape(bsz, seq, d_sb), row_vec(sb_out_g[l]), tri_sb, bdm)
        rw_t = router_w[l].T
        rwh = rw_t.astype(BF)
        rwl = (rw_t - rwh.astype(F32)).astype(BF)
        w_o = w_out[l].astype(BF)
        x_mid, h2, top_idx, top_gate = _out_proj(
            y_r.reshape(n_tok, d_r), y_s.reshape(n_tok, d_sb), x2, mod, row_vec(norm2_g[l]),
            w_o[:d_r], w_o[d_r:], rwh, rwl, router_b[l].reshape(N_EXPERTS, 1), seq)
        ys_by_slot = _moe_ffn(l, h2, top_idx[:TOP_K], top_gate[:TOP_K], exp_w1, exp_b1, exp_w2, exp_b2)
        x2 = _combine(x_mid, mod, ys_by_slot, seq)
    return x2.reshape(bsz, seq, d)


def kernel(x, c, norm1_g, norm2_g, ada_w, ada_b, w_in, shift_mu, decay_w0, decay_up, iclr_a0, iclr_up,
           gate_up, k_k, k_a, r_k, lnx_w, lnx_b, vres_down, vres_up, vres_b, q_norm_g, k_norm_g,
           sb_out_g, w_out, router_w, router_b, exp_w1, exp_b1, exp_w2, exp_b2):
    return _forward(x, c, norm1_g, norm2_g, ada_w, ada_b, w_in, shift_mu, decay_w0, decay_up, iclr_a0,
                    iclr_up, gate_up, k_k, k_a, r_k, lnx_w, lnx_b, vres_down, vres_up, vres_b,
                    q_norm_g, k_norm_g, sb_out_g, w_out, router_w, router_b, exp_w1, exp_b1,
                    exp_w2, exp_b2)
```

```python
import functools
import math

import jax
import jax.numpy as jnp
from jax import lax
from jax.experimental import pallas as pl
from jax.experimental.pallas import tpu as pltpu

F32 = jnp.float32
BF = jnp.bfloat16

HEAD_DIM = 64
LANES = 128
RWKV_CHUNK = 64
RWKV_SEQS = 2
DECAY_LORA = 64
ICLR_LORA = 64
GATE_LORA = 128
N_EXPERTS = 32
TOP_K = 4
SWIGLU_ALPHA = 1.702
SWIGLU_LIMIT = 7.0
NORM_EPS = 1e-6
GN_EPS = 1e-5 * HEAD_DIM
L2_EPS = 1e-12

ROW_TILE = 512
SB_TILE = 256
MOE_TILE = 512
VMEM_LIMIT = 48 * 1024 * 1024
MOE_VMEM_LIMIT = 58 * 1024 * 1024
LOG2E = 1.4426950408889634
SB_DEAD_MASS = 150.0


def _pack_bf16_pairs(x):
    n = x.shape[1] // 2
    lo = lax.bitcast_convert_type(x[:, :n].astype(BF).astype(F32), jnp.uint32)
    hi = lax.bitcast_convert_type(x[:, n:].astype(BF).astype(F32), jnp.uint32)
    return hi | (lo >> 16)


def _unpack_bf16_pairs(w):
    lo = lax.bitcast_convert_type(w << 16, F32)
    hi = lax.bitcast_convert_type(w & jnp.uint32(0xFFFF0000), F32)
    return lo, hi


def _dot(a, b):
    return jnp.dot(a, b, preferred_element_type=F32)


def _dot_nt(a, b):
    return lax.dot_general(a, b, (((1,), (1,)), ((), ())), preferred_element_type=F32)


def _split2(x):
    hi = x.astype(BF)
    lo = (x - hi.astype(F32)).astype(BF)
    return hi, lo


def _dot_lhs2(x, m_bf):
    hi, lo = _split2(x)
    return _dot(hi, m_bf) + _dot(lo, m_bf)


def _dot3(a, b):
    ah, al = _split2(a)
    bh, bl = _split2(b)
    return _dot(ah, bh) + (_dot(ah, bl) + _dot(al, bh))


def _softplus(x):
    return jnp.maximum(x, 0.0) + jnp.log(1.0 + jnp.exp(-jnp.abs(x)))


def _softplus2(x):
    return jnp.maximum(x, 0.0) + jnp.log2(1.0 + jnp.exp2(-jnp.abs(x)))


def _params(*sem):
    return pltpu.CompilerParams(dimension_semantics=sem, vmem_limit_bytes=VMEM_LIMIT)


def _ada_kernel(c_ref, w_ref, b_ref, o_ref):
    c = c_ref[...]
    cond = c * jax.nn.sigmoid(c)
    o_ref[0] = _dot3(cond, w_ref[0]) + b_ref[0]


def _ada_mod(c, ada_w, ada_b):
    depth, d, d6 = ada_w.shape
    bsz = c.shape[0]
    out = pl.pallas_call(
        _ada_kernel,
        grid=(depth, d6 // d),
        in_specs=[
            pl.BlockSpec((bsz, d), lambda l, j: (0, 0)),
            pl.BlockSpec((1, d, d), lambda l, j: (l, 0, j)),
            pl.BlockSpec((1, 1, d), lambda l, j: (l, 0, j)),
        ],
        out_specs=pl.BlockSpec((1, bsz, d), lambda l, j: (l, 0, j)),
        out_shape=jax.ShapeDtypeStruct((depth, bsz, d6), F32),
        compiler_params=_params("arbitrary", "arbitrary"),
        name="ada_mod",
    )(c, ada_w, ada_b.reshape(depth, 1, d6))
    return out.reshape(depth, bsz, d6 // d, d)


def _in_proj_kernel(c_rwkv, d_sb, x_ref, mod_ref, g_ref, w_ref, bdm_ref, qg_ref, kg_ref,
                    zr_ref, q_ref, k_ref, v_ref):
    x = x_ref[...]
    ms = jnp.mean(x * x, axis=-1, keepdims=True)
    h = x * lax.rsqrt(ms + NORM_EPS) * g_ref[...]
    h = h * (1.0 + mod_ref[0, 1:2, :]) + mod_ref[0, 0:1, :]
    z = _dot(h.astype(BF), w_ref[...])
    zr_ref[...] = z[:, :c_rwkv]
    bdm = bdm_ref[...]
    scale = LOG2E / math.sqrt(HEAD_DIM)
    for p in range(d_sb // LANES):
        lo = p * LANES
        q = z[:, c_rwkv + lo:c_rwkv + lo + LANES]
        k = z[:, c_rwkv + d_sb + lo:c_rwkv + d_sb + lo + LANES]
        qn = q * lax.rsqrt(_dot_lhs2(q * q, bdm) + NORM_EPS) * qg_ref[...]
        kn = k * lax.rsqrt(_dot_lhs2(k * k, bdm) + NORM_EPS) * kg_ref[...]
        q_ref[:, lo:lo + LANES] = (qn * scale).astype(BF)
        k_ref[:, lo:lo + LANES] = kn.astype(BF)
    v_ref[...] = z[:, c_rwkv + 2 * d_sb:].astype(BF)


def _in_proj(x2, mod, g, w_bf, bdm, qg, kg, seq, c_rwkv, d_sb):
    n, d = x2.shape
    c_in = w_bf.shape[1]
    tm = ROW_TILE
    per_seq = seq // tm
    row = lambda i: (i, 0)
    const = lambda i: (0, 0)
    return pl.pallas_call(
        functools.partial(_in_proj_kernel, c_rwkv, d_sb),
        grid=(n // tm,),
        in_specs=[
            pl.BlockSpec((tm, d), row),
            pl.BlockSpec((1,) + mod.shape[1:], lambda i: (i // per_seq, 0, 0)),
            pl.BlockSpec((1, d), const),
            pl.BlockSpec((d, c_in), const),
            pl.BlockSpec((LANES, LANES), const),
            pl.BlockSpec((1, LANES), const),
            pl.BlockSpec((1, LANES), const),
        ],
        out_specs=[
            pl.BlockSpec((tm, c_rwkv), row),
            pl.BlockSpec((tm, d_sb), row),
            pl.BlockSpec((tm, d_sb), row),
            pl.BlockSpec((tm, d_sb), row),
        ],
        out_shape=[
            jax.ShapeDtypeStruct((n, c_rwkv), F32),
            jax.ShapeDtypeStruct((n, d_sb), BF),
            jax.ShapeDtypeStruct((n, d_sb), BF),
            jax.ShapeDtypeStruct((n, d_sb), BF),
        ],
        compiler_params=_params("arbitrary"),
        name="in_proj",
    )(x2, mod, g, w_bf, bdm, qg, kg)


def _rwkv_kernel(has_vres, d_r, *refs):
    if has_vres:
        (z_ref, vf_ref, mu_ref, w0_ref, a0_ref, lora_ref, gup_ref, kk_ref, ka_ref, rk_ref, lnw_ref,
         lnb_ref, vd_ref, vu_ref, vb_ref, tri_ref, bdm_ref, o_ref, st_ref, prev_ref) = refs
    else:
        (z_ref, mu_ref, w0_ref, a0_ref, lora_ref, gup_ref, kk_ref, ka_ref, rk_ref, lnw_ref,
         lnb_ref, tri_ref, bdm_ref, o_ref, vo_ref, st_ref, prev_ref) = refs
    chunk = RWKV_CHUNK
    two = 2 * chunk
    nb = z_ref.shape[0]
    groups = d_r // LANES

    @pl.when(pl.program_id(1) == 0)
    def _():
        st_ref[...] = jnp.zeros_like(st_ref)
        prev_ref[...] = jnp.zeros_like(prev_ref)

    first = lax.broadcasted_iota(jnp.int32, (chunk, z_ref.shape[2]), 0) == 0
    mu = mu_ref[...]
    zs_parts = []
    for n in range(nb):
        z = z_ref[n]
        z_prev = jnp.where(first, prev_ref[n, 0:1, :], pltpu.roll(z, 1, axis=0))
        prev_ref[n, 0:1, :] = z[chunk - 1:chunk, :]
        zs_parts.append(z + (z_prev - z) * mu)
    zs = jnp.concatenate(zs_parts, axis=0)

    r = zs[:, :d_r]
    k = zs[:, d_r:2 * d_r]
    v = zs[:, 2 * d_r:3 * d_r]
    wa = zs[:, 3 * d_r:3 * d_r + LANES]
    g_lo = zs[:, 3 * d_r + LANES:]
    lane = lax.broadcasted_iota(jnp.int32, wa.shape, 1)
    feed = jnp.where(lane < DECAY_LORA, jnp.tanh(wa), wa)
    lora = _dot(feed.astype(BF), lora_ref[...])
    w_pre = w0_ref[...] + lora[:, :d_r]
    lw = -jnp.exp(-_softplus(-w_pre) - 0.5)
    a = jax.nn.sigmoid(a0_ref[...] + lora[:, d_r:])
    g = _dot(jax.nn.sigmoid(g_lo).astype(BF), gup_ref[...])
    if has_vres:
        vf = jnp.concatenate([vf_ref[n] for n in range(nb)], axis=0)
        low = _dot(v.astype(BF), vd_ref[...])
        v = v + (vf - v) * jax.nn.sigmoid(vb_ref[...] + _dot(low.astype(BF), vu_ref[...]))
    else:
        for n in range(nb):
            vo_ref[n] = v[n * chunk:(n + 1) * chunk]

    bdm = bdm_ref[...]
    kk = k * kk_ref[...]
    ssq = jnp.concatenate(
        [_dot_lhs2(jnp.square(kk[:, i * LANES:(i + 1) * LANES]), bdm) for i in range(groups)],
        axis=1) * float(HEAD_DIM)
    kk = kk / jnp.maximum(jnp.sqrt(ssq), L2_EPS)
    k = k * (1.0 + (a - 1.0) * ka_ref[...])

    tri = tri_ref[...]
    l1 = lw.astype(BF)
    rem = lw - l1.astype(F32)
    l2 = rem.astype(BF)
    l3 = (rem - l2.astype(F32)).astype(BF)
    cum_parts, total_parts = [], []
    for n in range(nb):
        rows = slice(n * chunk, (n + 1) * chunk)
        c = _dot(tri, l1[rows]) + (_dot(tri, l2[rows]) + _dot(tri, l3[rows]))
        cum_parts.append(c)
        total_parts.append(jnp.broadcast_to(c[chunk - 1:chunk, :], c.shape))
    cum = jnp.concatenate(cum_parts, axis=0)
    total = jnp.concatenate(total_parts, axis=0)
    grow = jnp.exp(-cum)
    to_end = jnp.exp(total - cum)
    b = kk * a
    a_src = -(kk * jnp.exp(cum - lw))
    r_src = r * jnp.exp(cum)
    b_src = b * grow
    k_src = k * grow
    b_end = b * to_end
    k_end = k * to_end
    end_decay = jnp.exp(total)

    row = lax.broadcasted_iota(jnp.int32, (two, LANES), 0)
    col = lax.broadcasted_iota(jnp.int32, (two, LANES), 1)
    same_head = (row < chunk) == (col < HEAD_DIM)
    row_in = row & (chunk - 1)
    col_in = col & (chunk - 1)
    strict = row_in > col_in
    incl = row_in >= col_in
    eye = (row == col).astype(F32)

    chains = [(n, p) for n in range(nb) for p in range(groups)]

    def part(t, n, p):
        return t[n * chunk:(n + 1) * chunk, p * LANES:(p + 1) * LANES]

    def bd(t, n, p):
        s = part(t, n, p)
        return jnp.where(same_head, jnp.concatenate([s, s], axis=0), 0.0)

    a_s = [bd(a_src, n, p).astype(BF) for n, p in chains]
    r_s = [bd(r_src, n, p).astype(BF) for n, p in chains]
    b_s = [bd(b_src, n, p).astype(BF) for n, p in chains]
    k_s = [bd(k_src, n, p).astype(BF) for n, p in chains]
    v_bd = [bd(v, n, p).astype(BF) for n, p in chains]
    st = [st_ref[n, p] for n, p in chains]
    st_b = [s.astype(BF) for s in st]

    pair = [_dot_nt(jnp.concatenate([a_s[c], r_s[c]], axis=0),
                    jnp.concatenate([b_s[c], k_s[c]], axis=0)) for c in range(len(chains))]
    n_ab = [jnp.where(strict, m[:two, :two], 0.0) for m in pair]
    a_ak = [jnp.where(strict, m[:two, two:], 0.0).astype(BF) for m in pair]
    a_rb = [jnp.where(incl, m[two:, :two], 0.0).astype(BF) for m in pair]
    a_rk = [jnp.where(incl, m[two:, two:], 0.0).astype(BF) for m in pair]

    inv = [eye + m for m in n_ab]
    power = [m.astype(BF) for m in n_ab]
    for _ in range(int(math.log2(chunk)) - 1):
        power = [_dot(m, m).astype(BF) for m in power]
        inv = [i + _dot(i.astype(BF), m) for i, m in zip(inv, power)]

    x0 = [_dot(jnp.concatenate([a_s[c], a_ak[c]], axis=1),
               jnp.concatenate([st_b[c], v_bd[c]], axis=0)) for c in range(len(chains))]
    u_b = [_dot(inv[c].astype(BF), x0[c].astype(BF)).astype(BF) for c in range(len(chains))]
    y_bd = [_dot(jnp.concatenate([r_s[c], a_rb[c], a_rk[c]], axis=1),
                 jnp.concatenate([st_b[c], u_b[c], v_bd[c]], axis=0)) for c in range(len(chains))]
    for c, (n, p) in enumerate(chains):
        bk_t = jnp.concatenate([bd(b_end, n, p), bd(k_end, n, p)], axis=0).T.astype(BF)
        decay_col = jnp.concatenate([part(end_decay, n, p)] * 2, axis=0).T
        st_ref[n, p] = decay_col * st[c] + _dot(bk_t, jnp.concatenate([u_b[c], v_bd[c]], axis=0))

    y = jnp.concatenate(
        [jnp.concatenate([y_bd[n * groups + p][:chunk] + y_bd[n * groups + p][chunk:]
                          for p in range(groups)], axis=1) for n in range(nb)], axis=0)
    rkr = r * k * rk_ref[...]
    outs = []
    for p in range(groups):
        sl = slice(p * LANES, (p + 1) * LANES)
        yp = y[:, sl]
        cen = yp - _dot_lhs2(yp, bdm)
        var = _dot_lhs2(cen * cen, bdm)
        yn = cen * lax.rsqrt(var + GN_EPS) * lnw_ref[:, sl] + lnb_ref[:, sl]
        bonus = _dot_lhs2(rkr[:, sl], bdm) * float(HEAD_DIM) * v[:, sl]
        outs.append((yn + bonus) * g[:, sl])
    out = jnp.concatenate(outs, axis=1).astype(o_ref.dtype)
    for n in range(nb):
        o_ref[n] = out[n * chunk:(n + 1) * chunk]


def _rwkv_mix(z, v_first, mu, w0, a0, lora_up, gate_up, k_k, k_a, rk, lnw, lnb, vres, tri, bdm):
    bsz, seq, c_rwkv = z.shape
    d_r = w0.shape[1]
    chunk = RWKV_CHUNK
    nb = RWKV_SEQS
    has_vres = vres is not None
    tok_z = pl.BlockSpec((nb, chunk, c_rwkv), lambda b, c: (b, c, 0))
    tok = pl.BlockSpec((nb, chunk, d_r), lambda b, c: (b, c, 0))
    full = lambda arr: pl.BlockSpec(arr.shape, lambda b, c: (0,) * arr.ndim)
    consts = [mu, w0, a0, lora_up, gate_up, k_k, k_a, rk, lnw, lnb]
    args = [z] + ([v_first] if has_vres else []) + consts + (list(vres) if has_vres else []) + [tri, bdm]
    in_specs = ([tok_z] + ([tok] if has_vres else []) + [full(t) for t in consts]
                + ([full(t) for t in vres] if has_vres else []) + [full(tri), full(bdm)])
    out_specs = [tok] if has_vres else [tok, tok]
    out_shape = [jax.ShapeDtypeStruct((bsz, seq, d_r), BF)]
    if not has_vres:
        out_shape.append(jax.ShapeDtypeStruct((bsz, seq, d_r), F32))
    res = pl.pallas_call(
        functools.partial(_rwkv_kernel, has_vres, d_r),
        grid=(bsz // nb, seq // chunk),
        in_specs=in_specs,
        out_specs=out_specs,
        out_shape=out_shape,
        scratch_shapes=[pltpu.VMEM((nb, d_r // LANES, LANES, LANES), F32),
                        pltpu.VMEM((nb, 8, c_rwkv), F32)],
        compiler_params=_params("arbitrary", "arbitrary"),
        name="rwkv_mix",
    )(*args)
    return (res[0], v_first) if has_vres else (res[0], res[1])


def _sb_kernel(q_ref, k_ref, v_ref, og_ref, tri_ref, bdm_ref, o_ref, acc_ref, carry_ref):
    t = SB_TILE
    qi = pl.program_id(2)
    q = q_ref[0]
    lane = lax.broadcasted_iota(jnp.int32, (t, LANES), 1)
    rowi = lax.broadcasted_iota(jnp.int32, (t, t), 0)
    coli = lax.broadcasted_iota(jnp.int32, (t, t), 1)
    causal = coli < rowi
    zero = jnp.zeros_like(q)
    q_heads = [jnp.where(lane < HEAD_DIM, q, zero), jnp.where(lane >= HEAD_DIM, q, zero)]
    tri = tri_ref[...]

    def scores(j, diagonal):
        start = pl.multiple_of(j * t, t)
        kb = k_ref[0, pl.ds(start, t), :]
        vb = v_ref[0, pl.ds(start, t), :]
        logits = [_dot_nt(q_heads[h], kb) for h in range(2)]
        soft = [_softplus2(s) for s in logits]
        drop = [jnp.where(causal, s, 0.0) for s in soft] if diagonal else soft
        sums = [_dot(d.astype(BF), tri) for d in drop]
        args = [logits[h] - soft[h] - sums[h][:, :t] for h in range(2)]
        return args, [s[:, t:] for s in sums], vb

    def absorb(block, state, diagonal):
        args, totals, vb = block
        out = []
        for h in range(2):
            if diagonal:
                w = jnp.where(causal, jnp.exp2(args[h]), 0.0)
                out.append((_dot(w.astype(BF), vb), totals[h]))
            else:
                acc, carry = state[h]
                w = jnp.exp2(args[h] - jnp.concatenate([carry] * (t // LANES), axis=1))
                out.append((acc + _dot(w.astype(BF), vb), carry + totals[h]))
        return out

    def save(state):
        for h in range(2):
            acc_ref[h], carry_ref[h] = state[h]

    def lightest(state):
        return jnp.minimum(jnp.min(state[0][1]), jnp.min(state[1][1]))

    @pl.when(qi == 0)
    def _():
        save(absorb(scores(qi, True), None, True))

    @pl.when(qi > 0)
    def _():
        newest = scores(qi, True)
        older = scores(qi - 1, False)
        save(absorb(older, absorb(newest, None, True), False))

    def more(state):
        i, mass = state
        return jnp.logical_and(i < qi, mass < SB_DEAD_MASS)

    def body(state):
        i, _ = state
        new = absorb(scores(qi - 1 - i, False), [(acc_ref[h], carry_ref[h]) for h in range(2)], False)
        save(new)
        return i + 1, lightest(new)

    first_mass = jnp.minimum(jnp.min(carry_ref[0]), jnp.min(carry_ref[1]))
    lax.while_loop(more, body, (jnp.int32(1), first_mass))

    o = jnp.where(lane < HEAD_DIM, acc_ref[0], acc_ref[1])
    ms = _dot_lhs2(o * o, bdm_ref[...])
    o_ref[0] = (o * lax.rsqrt(ms + NORM_EPS) * og_ref[...]).astype(o_ref.dtype)


def _sb_attention(q, k, v, og, tri2, bdm):
    bsz, seq, ds = q.shape
    t = SB_TILE
    return pl.pallas_call(
        _sb_kernel,
        grid=(bsz, ds // LANES, seq // t),
        in_specs=[
            pl.BlockSpec((1, t, LANES), lambda b, p, i: (b, i, p)),
            pl.BlockSpec((1, seq, LANES), lambda b, p, i: (b, 0, p)),
            pl.BlockSpec((1, seq, LANES), lambda b, p, i: (b, 0, p)),
            pl.BlockSpec((1, LANES), lambda b, p, i: (0, p)),
            pl.BlockSpec((t, t + LANES), lambda b, p, i: (0, 0)),
            pl.BlockSpec((LANES, LANES), lambda b, p, i: (0, 0)),
        ],
        out_specs=pl.BlockSpec((1, t, LANES), lambda b, p, i: (b, i, p)),
        out_shape=jax.ShapeDtypeStruct((bsz, seq, ds), BF),
        scratch_shapes=[pltpu.VMEM((2, t, LANES), F32), pltpu.VMEM((2, t, LANES), F32)],
        compiler_params=_params("arbitrary", "arbitrary", "arbitrary"),
        name="sb_attention",
    )(q, k, v, og, tri2, bdm)


def _out_proj_kernel(yr_ref, ys_ref, x_ref, mod_ref, g_ref, wr_ref, ws_ref, rwh_ref, rwl_ref, rb_ref,
                     xo_ref, h_ref, idx_ref, gate_ref):
    mixed = _dot(yr_ref[...], wr_ref[...]) + _dot(ys_ref[...], ws_ref[...])
    x = x_ref[...] + mod_ref[0, 2:3, :] * mixed
    xo_ref[...] = x
    ms = jnp.mean(x * x, axis=-1, keepdims=True)
    h = x * lax.rsqrt(ms + NORM_EPS) * g_ref[...]
    h = h * (1.0 + mod_ref[0, 4:5, :]) + mod_ref[0, 3:4, :]
    h_ref[...] = _pack_bf16_pairs(h)

    hh, hl = _split2(h)
    rwh = rwh_ref[...]
    logits = _dot_nt(rwh, hh) + (_dot_nt(rwh, hl) + _dot_nt(rwl_ref[...], hh)) + rb_ref[...]
    eidx = lax.broadcasted_iota(jnp.int32, logits.shape, 0)
    tops, idxs = [], []
    for _ in range(TOP_K):
        m = jnp.max(logits, axis=0, keepdims=True)
        idx = jnp.min(jnp.where(logits == m, eidx, N_EXPERTS), axis=0, keepdims=True)
        tops.append(m)
        idxs.append(idx)
        logits = jnp.where(eidx == idx, -jnp.inf, logits)
    exps = [jnp.exp(m - tops[0]) for m in tops]
    den = exps[0] + exps[1] + exps[2] + exps[3]
    pad = jnp.zeros((LANES - TOP_K, den.shape[1]), F32)
    gate_ref[...] = jnp.concatenate([e / den for e in exps] + [pad], axis=0).T
    idx_ref[...] = jnp.concatenate(idxs + [jnp.zeros_like(idxs[0])] * (8 - TOP_K), axis=0)


def _out_proj(yr, ys, x2, mod, g, wr, ws, rwh, rwl, rb, seq):
    n, d = x2.shape
    dr = yr.shape[1]
    ds = ys.shape[1]
    tm = ROW_TILE
    per_seq = seq // tm
    row = lambda i: (i, 0)
    const = lambda i: (0, 0)
    colblk = lambda i: (0, i)
    return pl.pallas_call(
        _out_proj_kernel,
        grid=(n // tm,),
        in_specs=[
            pl.BlockSpec((tm, dr), row),
            pl.BlockSpec((tm, ds), row),
            pl.BlockSpec((tm, d), row),
            pl.BlockSpec((1,) + mod.shape[1:], lambda i: (i // per_seq, 0, 0)),
            pl.BlockSpec((1, d), const),
            pl.BlockSpec((dr, d), const),
            pl.BlockSpec((ds, d), const),
            pl.BlockSpec((N_EXPERTS, d), const),
            pl.BlockSpec((N_EXPERTS, d), const),
            pl.BlockSpec((N_EXPERTS, 1), const),
        ],
        out_specs=[
            pl.BlockSpec((tm, d), row),
            pl.BlockSpec((tm, d // 2), row),
            pl.BlockSpec((8, tm), colblk),
            pl.BlockSpec((tm, LANES), row),
        ],
        out_shape=[
            jax.ShapeDtypeStruct((n, d), F32),
            jax.ShapeDtypeStruct((n, d // 2), jnp.uint32),
            jax.ShapeDtypeStruct((8, n), jnp.int32),
            jax.ShapeDtypeStruct((n, LANES), F32),
        ],
        compiler_params=_params("arbitrary"),
        name="out_proj_router",
    )(yr, ys, x2, mod, g, wr, ws, rwh, rwl, rb)


def _moe_kernel(te_ref, nu_ref, x_ref, w1_ref, b1_ref, w2_ref, b2_ref, o_ref, w1b_ref, w2b_ref):
    i = pl.program_id(0)
    f = w2_ref.shape[2]
    live = i < nu_ref[0]
    new_expert = jnp.logical_or(i == 0, te_ref[i] != te_ref[jnp.maximum(i - 1, 0)])

    @pl.when(jnp.logical_and(live, new_expert))
    def _():
        w1b_ref[...] = w1_ref[0, 0].astype(BF)
        w2b_ref[...] = w2_ref[0, 0].astype(BF)

    @pl.when(live)
    def _():
        lo, hi = _unpack_bf16_pairs(x_ref[...])
        x = jnp.concatenate([lo.astype(BF), hi.astype(BF)], axis=1)
        hid = _dot(x, w1b_ref[...]) + b1_ref[0, 0]
        glu = jnp.minimum(hid[:, :f], SWIGLU_LIMIT)
        lin = jnp.clip(hid[:, f:], -SWIGLU_LIMIT, SWIGLU_LIMIT)
        act = glu * jax.nn.sigmoid(SWIGLU_ALPHA * glu) * (lin + 1.0)
        o_ref[...] = _pack_bf16_pairs(_dot(act.astype(BF), w2b_ref[...]) + b2_ref[0, 0])

    @pl.when(jnp.logical_not(live))
    def _():
        o_ref[...] = jnp.zeros_like(o_ref)


def _moe_experts(layer, tile_expert, n_used, xs, w1, b1, w2, b2):
    n_rows, half = xs.shape
    _, _, d, f2 = w1.shape
    f = w2.shape[2]
    tm = MOE_TILE
    expert = lambda i, te, nu: (layer, te[i], 0, 0)
    return pl.pallas_call(
        _moe_kernel,
        grid_spec=pltpu.PrefetchScalarGridSpec(
            num_scalar_prefetch=2,
            grid=(n_rows // tm,),
            in_specs=[
                pl.BlockSpec((tm, half), lambda i, te, nu: (i, 0)),
                pl.BlockSpec((1, 1, d, f2), expert),
                pl.BlockSpec((1, 1, 1, f2), expert),
                pl.BlockSpec((1, 1, f, d), expert),
                pl.BlockSpec((1, 1, 1, d), expert),
            ],
            out_specs=pl.BlockSpec((tm, half), lambda i, te, nu: (i, 0)),
            scratch_shapes=[pltpu.VMEM((d, f2), BF), pltpu.VMEM((f, d), BF)],
        ),
        out_shape=jax.ShapeDtypeStruct((n_rows, half), jnp.uint32),
        compiler_params=pltpu.CompilerParams(dimension_semantics=("arbitrary",),
                                             vmem_limit_bytes=MOE_VMEM_LIMIT),
        name="moe_experts",
    )(tile_expert, n_used, xs, w1, b1.reshape(b1.shape[:2] + (1, f2)), w2,
      b2.reshape(b2.shape[:2] + (1, d)))


def _combine_kernel(x_ref, mod_ref, gate_ref, y0_ref, y1_ref, y2_ref, y3_ref, o_ref):
    half = y0_ref.shape[1]
    lo = hi = None
    for s, y_ref in enumerate((y0_ref, y1_ref, y2_ref, y3_ref)):
        gate = gate_ref[:, s:s + 1]
        l2, h2 = _unpack_bf16_pairs(y_ref[...])
        lo = gate * l2 if lo is None else lo + gate * l2
        hi = gate * h2 if hi is None else hi + gate * h2
    o_ref[:, :half] = x_ref[:, :half] + mod_ref[0, 5:6, :half] * lo
    o_ref[:, half:] = x_ref[:, half:] + mod_ref[0, 5:6, half:] * hi


def _combine(x_mid, mod, gate_rows, ys_by_slot, seq):
    n, d = x_mid.shape
    tm = ROW_TILE
    per_seq = seq // tm
    row = lambda i: (i, 0)
    return pl.pallas_call(
        _combine_kernel,
        grid=(n // tm,),
        in_specs=[pl.BlockSpec((tm, d), row),
                  pl.BlockSpec((1,) + mod.shape[1:], lambda i: (i // per_seq, 0, 0)),
                  pl.BlockSpec((tm, LANES), row)]
                 + [pl.BlockSpec((tm, d // 2), row)] * TOP_K,
        out_specs=pl.BlockSpec((tm, d), row),
        out_shape=jax.ShapeDtypeStruct((n, d), F32),
        compiler_params=_params("arbitrary"),
        name="moe_combine",
    )(x_mid, mod, gate_rows, *ys_by_slot)


def _moe_ffn(layer, h_packed, top_idx, w1, b1, w2, b2):
    n_tok = h_packed.shape[0]
    tm = MOE_TILE
    n_assign = n_tok * TOP_K
    n_rows = n_assign + N_EXPERTS * tm
    n_tiles = n_rows // tm
    flat_e = top_idx.reshape(-1)
    order = jnp.argsort(flat_e, stable=True)
    rank_of = jnp.argsort(order)
    counts = jnp.sum((flat_e[:, None] == jnp.arange(N_EXPERTS, dtype=jnp.int32)[None, :]).astype(jnp.int32),
                     axis=0)
    padded = (counts + tm - 1) // tm * tm
    padded_end = jnp.cumsum(padded)
    padded_start = padded_end - padded
    start = jnp.cumsum(counts) - counts
    shift = padded_start - start
    pos = (rank_of + shift[flat_e]).reshape(TOP_K, n_tok)
    tile_start = jnp.arange(n_tiles, dtype=jnp.int32) * tm
    tile_expert = jnp.minimum(
        jnp.sum((tile_start[:, None] >= padded_end[None, :]).astype(jnp.int32), axis=1),
        N_EXPERTS - 1).astype(jnp.int32)
    n_used = (padded_end[-1:] // tm).astype(jnp.int32)
    within = jnp.arange(tm, dtype=jnp.int32)[None, :]
    local = (tile_start - padded_start[tile_expert])[:, None] + within
    src = jnp.clip(start[tile_expert][:, None] + local, 0, n_assign - 1)
    valid = local < counts[tile_expert][:, None]
    row_tok = jnp.where(valid, order[src] % n_tok, (tile_start[:, None] + within) % n_tok)
    ys = _moe_experts(layer, tile_expert, n_used, h_packed[row_tok.reshape(-1)], w1, b1, w2, b2)
    return [ys[pos[s]] for s in range(TOP_K)]


def _block_diag_mean():
    i = jnp.arange(LANES)
    return jnp.where((i[:, None] // HEAD_DIM) == (i[None, :] // HEAD_DIM), 1.0 / HEAD_DIM, 0.0).astype(BF)


def _block_diag2(a, b):
    top = jnp.concatenate([a, jnp.zeros((a.shape[0], b.shape[1]), a.dtype)], axis=1)
    bot = jnp.concatenate([jnp.zeros((b.shape[0], a.shape[1]), b.dtype), b], axis=1)
    return jnp.concatenate([top, bot], axis=0)


@jax.jit
def _forward(x, c, norm1_g, norm2_g, ada_w, ada_b, w_in, shift_mu, decay_w0, decay_up, iclr_a0,
             iclr_up, gate_up, k_k, k_a, r_k, lnx_w, lnx_b, vres_down, vres_up, vres_b, q_norm_g,
             k_norm_g, sb_out_g, w_out, router_w, router_b, exp_w1, exp_b1, exp_w2, exp_b2):
    bsz, seq, d = x.shape
    depth = w_in.shape[0]
    d_r = decay_w0.shape[1]
    c_rwkv = shift_mu.shape[1]
    d_sb = sb_out_g.shape[1]
    n_tok = bsz * seq
    assert seq % ROW_TILE == 0 and seq % SB_TILE == 0 and seq % RWKV_CHUNK == 0
    assert bsz % RWKV_SEQS == 0 and DECAY_LORA + ICLR_LORA == LANES and GATE_LORA == LANES

    bdm = _block_diag_mean()
    ci = jnp.arange(RWKV_CHUNK)
    tri_chunk = (ci[None, :] <= ci[:, None]).astype(BF)
    si = jnp.arange(SB_TILE)
    tri_sb = jnp.concatenate([(si[:, None] > si[None, :]).astype(BF),
                              jnp.ones((SB_TILE, LANES), BF)], axis=1)
    row_vec = lambda t: t.reshape(1, -1)

    mod_all = _ada_mod(c, ada_w, ada_b)
    x2 = x.reshape(n_tok, d)
    v_first = None
    for l in range(depth):
        mod = mod_all[l]
        zr, q, k, v = _in_proj(
            x2, mod, row_vec(norm1_g[l]), w_in[l].astype(BF), bdm,
            row_vec(jnp.tile(q_norm_g[l], LANES // HEAD_DIM)),
            row_vec(jnp.tile(k_norm_g[l], LANES // HEAD_DIM)), seq, c_rwkv, d_sb)
        if l == 0:
            vres = None
        else:
            rank = vres_down.shape[2]
            vres = (jnp.pad(vres_down[l - 1], ((0, 0), (0, LANES - rank))).astype(BF),
                    jnp.pad(vres_up[l - 1], ((0, LANES - rank), (0, 0))).astype(BF),
                    row_vec(vres_b[l - 1]))
        y_r, v_first = _rwkv_mix(
            zr.reshape(bsz, seq, c_rwkv), v_first, row_vec(shift_mu[l]), row_vec(decay_w0[l]),
            row_vec(iclr_a0[l]), _block_diag2(decay_up[l], iclr_up[l]).astype(BF),
            gate_up[l].astype(BF), row_vec(k_k[l]), row_vec(k_a[l]), row_vec(r_k[l]),
            row_vec(lnx_w[l]), row_vec(lnx_b[l]), vres, tri_chunk, bdm)
        y_s = _sb_attention(q.reshape(bsz, seq, d_sb), k.reshape(bsz, seq, d_sb),
                            v.reshape(bsz, seq, d_sb), row_vec(sb_out_g[l]), tri_sb, bdm)
        rw_t = router_w[l].T
        rwh = rw_t.astype(BF)
        rwl = (rw_t - rwh.astype(F32)).astype(BF)
        w_o = w_out[l].astype(BF)
        x_mid, h2, top_idx, gate_rows = _out_proj(
            y_r.reshape(n_tok, d_r), y_s.reshape(n_tok, d_sb), x2, mod, row_vec(norm2_g[l]),
            w_o[:d_r], w_o[d_r:], rwh, rwl, router_b[l].reshape(N_EXPERTS, 1), seq)
        ys_by_slot = _moe_ffn(l, h2, top_idx[:TOP_K], exp_w1, exp_b1, exp_w2, exp_b2)
        x2 = _combine(x_mid, mod, gate_rows, ys_by_slot, seq)
    return x2.reshape(bsz, seq, d)


def kernel(x, c, norm1_g, norm2_g, ada_w, ada_b, w_in, shift_mu, decay_w0, decay_up, iclr_a0, iclr_up,
           gate_up, k_k, k_a, r_k, lnx_w, lnx_b, vres_down, vres_up, vres_b, q_norm_g, k_norm_g,
           sb_out_g, w_out, router_w, router_b, exp_w1, exp_b1, exp_w2, exp_b2):
    return _forward(x, c, norm1_g, norm2_g, ada_w, ada_b, w_in, shift_mu, decay_w0, decay_up, iclr_a0,
                    iclr_up, gate_up, k_k, k_a, r_k, lnx_w, lnx_b, vres_down, vres_up, vres_b,
                    q_norm_g, k_norm_g, sb_out_g, w_out, router_w, router_b, exp_w1, exp_b1,
                    exp_w2, exp_b2)
```

```python
import functools
import math

import jax
import jax.numpy as jnp
from jax import lax
from jax.experimental import pallas as pl
from jax.experimental.pallas import tpu as pltpu

F32 = jnp.float32
BF = jnp.bfloat16

HEAD_DIM = 64
LANES = 128
RWKV_CHUNK = 64
RWKV_SEQS = 2
DECAY_LORA = 64
ICLR_LORA = 64
GATE_LORA = 128
N_EXPERTS = 32
TOP_K = 4
SWIGLU_ALPHA = 1.702
SWIGLU_LIMIT = 7.0
NORM_EPS = 1e-6
GN_EPS = 1e-5 * HEAD_DIM
L2_EPS = 1e-12

ROW_TILE = 512
SB_TILE = 256
MOE_TILE = 512
VMEM_LIMIT = 48 * 1024 * 1024
MOE_VMEM_LIMIT = 58 * 1024 * 1024
LOG2E = 1.4426950408889634
SB_DEAD_MASS = 150.0


def _pack_bf16_pairs(x):
    n = x.shape[1] // 2
    lo = lax.bitcast_convert_type(x[:, :n].astype(BF).astype(F32), jnp.uint32)
    hi = lax.bitcast_convert_type(x[:, n:].astype(BF).astype(F32), jnp.uint32)
    return hi | (lo >> 16)


def _unpack_bf16_pairs(w):
    lo = lax.bitcast_convert_type(w << 16, F32)
    hi = lax.bitcast_convert_type(w & jnp.uint32(0xFFFF0000), F32)
    return lo, hi


def _dot(a, b):
    return jnp.dot(a, b, preferred_element_type=F32)


def _dot_nt(a, b):
    return lax.dot_general(a, b, (((1,), (1,)), ((), ())), preferred_element_type=F32)


def _split2(x):
    hi = x.astype(BF)
    lo = (x - hi.astype(F32)).astype(BF)
    return hi, lo


def _dot_lhs2(x, m_bf):
    hi, lo = _split2(x)
    return _dot(hi, m_bf) + _dot(lo, m_bf)


def _dot3(a, b):
    ah, al = _split2(a)
    bh, bl = _split2(b)
    return _dot(ah, bh) + (_dot(ah, bl) + _dot(al, bh))


def _softplus(x):
    return jnp.maximum(x, 0.0) + jnp.log(1.0 + jnp.exp(-jnp.abs(x)))


def _softplus2(x):
    return jnp.maximum(x, 0.0) + jnp.log2(1.0 + jnp.exp2(-jnp.abs(x)))


def _params(*sem):
    return pltpu.CompilerParams(dimension_semantics=sem, vmem_limit_bytes=VMEM_LIMIT)


def _ada_kernel(c_ref, w_ref, b_ref, o_ref):
    c = c_ref[...]
    cond = c * jax.nn.sigmoid(c)
    o_ref[0] = _dot3(cond, w_ref[0]) + b_ref[0]


def _ada_mod(c, ada_w, ada_b):
    depth, d, d6 = ada_w.shape
    bsz = c.shape[0]
    out = pl.pallas_call(
        _ada_kernel,
        grid=(depth, d6 // d),
        in_specs=[
            pl.BlockSpec((bsz, d), lambda l, j: (0, 0)),
            pl.BlockSpec((1, d, d), lambda l, j: (l, 0, j)),
            pl.BlockSpec((1, 1, d), lambda l, j: (l, 0, j)),
        ],
        out_specs=pl.BlockSpec((1, bsz, d), lambda l, j: (l, 0, j)),
        out_shape=jax.ShapeDtypeStruct((depth, bsz, d6), F32),
        compiler_params=_params("arbitrary", "arbitrary"),
        name="ada_mod",
    )(c, ada_w, ada_b.reshape(depth, 1, d6))
    return out.reshape(depth, bsz, d6 // d, d)


def _in_proj_kernel(c_rwkv, d_sb, x_ref, mod_ref, g_ref, w_ref, bdm_ref, qg_ref, kg_ref,
                    zr_ref, q_ref, k_ref, v_ref):
    x = x_ref[...]
    ms = jnp.mean(x * x, axis=-1, keepdims=True)
    h = x * lax.rsqrt(ms + NORM_EPS) * g_ref[...]
    h = h * (1.0 + mod_ref[0, 1:2, :]) + mod_ref[0, 0:1, :]
    z = _dot(h.astype(BF), w_ref[...])
    zr_ref[...] = z[:, :c_rwkv]
    bdm = bdm_ref[...]
    scale = LOG2E / math.sqrt(HEAD_DIM)
    for p in range(d_sb // LANES):
        lo = p * LANES
        q = z[:, c_rwkv + lo:c_rwkv + lo + LANES]
        k = z[:, c_rwkv + d_sb + lo:c_rwkv + d_sb + lo + LANES]
        qn = q * lax.rsqrt(_dot_lhs2(q * q, bdm) + NORM_EPS) * qg_ref[...]
        kn = k * lax.rsqrt(_dot_lhs2(k * k, bdm) + NORM_EPS) * kg_ref[...]
        q_ref[:, lo:lo + LANES] = (qn * scale).astype(BF)
        k_ref[:, lo:lo + LANES] = kn.astype(BF)
    v_ref[...] = z[:, c_rwkv + 2 * d_sb:].astype(BF)


def _in_proj(x2, mod, g, w_bf, bdm, qg, kg, seq, c_rwkv, d_sb):
    n, d = x2.shape
    c_in = w_bf.shape[1]
    tm = ROW_TILE
    per_seq = seq // tm
    row = lambda i: (i, 0)
    const = lambda i: (0, 0)
    return pl.pallas_call(
        functools.partial(_in_proj_kernel, c_rwkv, d_sb),
        grid=(n // tm,),
        in_specs=[
            pl.BlockSpec((tm, d), row),
            pl.BlockSpec((1,) + mod.shape[1:], lambda i: (i // per_seq, 0, 0)),
            pl.BlockSpec((1, d), const),
            pl.BlockSpec((d, c_in), const),
            pl.BlockSpec((LANES, LANES), const),
            pl.BlockSpec((1, LANES), const),
            pl.BlockSpec((1, LANES), const),
        ],
        out_specs=[
            pl.BlockSpec((tm, c_rwkv), row),
            pl.BlockSpec((tm, d_sb), row),
            pl.BlockSpec((tm, d_sb), row),
            pl.BlockSpec((tm, d_sb), row),
        ],
        out_shape=[
            jax.ShapeDtypeStruct((n, c_rwkv), F32),
            jax.ShapeDtypeStruct((n, d_sb), BF),
            jax.ShapeDtypeStruct((n, d_sb), BF),
            jax.ShapeDtypeStruct((n, d_sb), BF),
        ],
        compiler_params=_params("arbitrary"),
        name="in_proj",
    )(x2, mod, g, w_bf, bdm, qg, kg)


def _rwkv_kernel(has_vres, d_r, *refs):
    if has_vres:
        (z_ref, vf_ref, mu_ref, w0_ref, a0_ref, lora_ref, gup_ref, kk_ref, ka_ref, rk_ref, lnw_ref,
         lnb_ref, vd_ref, vu_ref, vb_ref, tri_ref, bdm_ref, o_ref, st_ref, prev_ref) = refs
    else:
        (z_ref, mu_ref, w0_ref, a0_ref, lora_ref, gup_ref, kk_ref, ka_ref, rk_ref, lnw_ref,
         lnb_ref, tri_ref, bdm_ref, o_ref, vo_ref, st_ref, prev_ref) = refs
    chunk = RWKV_CHUNK
    two = 2 * chunk
    nb = z_ref.shape[0]
    groups = d_r // LANES

    @pl.when(pl.program_id(1) == 0)
    def _():
        st_ref[...] = jnp.zeros_like(st_ref)
        prev_ref[...] = jnp.zeros_like(prev_ref)

    first = lax.broadcasted_iota(jnp.int32, (chunk, z_ref.shape[2]), 0) == 0
    mu = mu_ref[...]
    zs_parts = []
    for n in range(nb):
        z = z_ref[n]
        z_prev = jnp.where(first, prev_ref[n, 0:1, :], pltpu.roll(z, 1, axis=0))
        prev_ref[n, 0:1, :] = z[chunk - 1:chunk, :]
        zs_parts.append(z + (z_prev - z) * mu)
    zs = jnp.concatenate(zs_parts, axis=0)

    r = zs[:, :d_r]
    k = zs[:, d_r:2 * d_r]
    v = zs[:, 2 * d_r:3 * d_r]
    wa = zs[:, 3 * d_r:3 * d_r + LANES]
    g_lo = zs[:, 3 * d_r + LANES:]
    lane = lax.broadcasted_iota(jnp.int32, wa.shape, 1)
    feed = jnp.where(lane < DECAY_LORA, jnp.tanh(wa), wa)
    lora = _dot(feed.astype(BF), lora_ref[...])
    w_pre = w0_ref[...] + lora[:, :d_r]
    lw = -jnp.exp(-_softplus(-w_pre) - 0.5)
    a = jax.nn.sigmoid(a0_ref[...] + lora[:, d_r:])
    g = _dot(jax.nn.sigmoid(g_lo).astype(BF), gup_ref[...])
    if has_vres:
        vf = jnp.concatenate([vf_ref[n] for n in range(nb)], axis=0)
        low = _dot(v.astype(BF), vd_ref[...])
        v = v + (vf - v) * jax.nn.sigmoid(vb_ref[...] + _dot(low.astype(BF), vu_ref[...]))
    else:
        for n in range(nb):
            vo_ref[n] = v[n * chunk:(n + 1) * chunk]

    bdm = bdm_ref[...]
    kk = k * kk_ref[...]
    ssq = jnp.concatenate(
        [_dot(jnp.square(kk[:, i * LANES:(i + 1) * LANES]).astype(BF), bdm) for i in range(groups)],
        axis=1) * float(HEAD_DIM)
    kk = kk / jnp.maximum(jnp.sqrt(ssq), L2_EPS)
    k = k * (1.0 + (a - 1.0) * ka_ref[...])

    tri = tri_ref[...]
    l1 = lw.astype(BF)
    rem = lw - l1.astype(F32)
    l2 = rem.astype(BF)
    l3 = (rem - l2.astype(F32)).astype(BF)
    cum_parts, total_parts = [], []
    for n in range(nb):
        rows = slice(n * chunk, (n + 1) * chunk)
        c = _dot(tri, l1[rows]) + (_dot(tri, l2[rows]) + _dot(tri, l3[rows]))
        cum_parts.append(c)
        total_parts.append(jnp.broadcast_to(c[chunk - 1:chunk, :], c.shape))
    cum = jnp.concatenate(cum_parts, axis=0)
    total = jnp.concatenate(total_parts, axis=0)
    grow = jnp.exp(-cum)
    to_end = jnp.exp(total - cum)
    b = kk * a
    a_src = -(kk * jnp.exp(cum - lw))
    r_src = r * jnp.exp(cum)
    b_src = b * grow
    k_src = k * grow
    b_end = b * to_end
    k_end = k * to_end
    end_decay = jnp.exp(total)

    row = lax.broadcasted_iota(jnp.int32, (two, LANES), 0)
    col = lax.broadcasted_iota(jnp.int32, (two, LANES), 1)
    same_head = (row < chunk) == (col < HEAD_DIM)
    row_in = row & (chunk - 1)
    col_in = col & (chunk - 1)
    strict = row_in > col_in
    incl = row_in >= col_in
    eye = (row == col).astype(F32)

    chains = [(n, p) for n in range(nb) for p in range(groups)]

    def part(t, n, p):
        return t[n * chunk:(n + 1) * chunk, p * LANES:(p + 1) * LANES]

    def bd(t, n, p):
        s = part(t, n, p)
        return jnp.where(same_head, jnp.concatenate([s, s], axis=0), 0.0)

    a_s = [bd(a_src, n, p).astype(BF) for n, p in chains]
    r_s = [bd(r_src, n, p).astype(BF) for n, p in chains]
    b_s = [bd(b_src, n, p).astype(BF) for n, p in chains]
    k_s = [bd(k_src, n, p).astype(BF) for n, p in chains]
    v_bd = [bd(v, n, p).astype(BF) for n, p in chains]
    st = [st_ref[n, p] for n, p in chains]
    st_b = [s.astype(BF) for s in st]

    pair = [_dot_nt(jnp.concatenate([a_s[c], r_s[c]], axis=0),
                    jnp.concatenate([b_s[c], k_s[c]], axis=0)) for c in range(len(chains))]
    n_ab = [jnp.where(strict, m[:two, :two], 0.0) for m in pair]
    a_ak = [jnp.where(strict, m[:two, two:], 0.0).astype(BF) for m in pair]
    a_rb = [jnp.where(incl, m[two:, :two], 0.0).astype(BF) for m in pair]
    a_rk = [jnp.where(incl, m[two:, two:], 0.0).astype(BF) for m in pair]

    inv = [eye + m for m in n_ab]
    power = [m.astype(BF) for m in n_ab]
    for _ in range(int(math.log2(chunk)) - 1):
        power = [_dot(m, m).astype(BF) for m in power]
        inv = [i + _dot(i.astype(BF), m) for i, m in zip(inv, power)]

    x0 = [_dot(jnp.concatenate([a_s[c], a_ak[c]], axis=1),
               jnp.concatenate([st_b[c], v_bd[c]], axis=0)) for c in range(len(chains))]
    u_b = [_dot(inv[c].astype(BF), x0[c].astype(BF)).astype(BF) for c in range(len(chains))]
    y_bd = [_dot(jnp.concatenate([r_s[c], a_rb[c], a_rk[c]], axis=1),
                 jnp.concatenate([st_b[c], u_b[c], v_bd[c]], axis=0)) for c in range(len(chains))]
    for c, (n, p) in enumerate(chains):
        bk_t = jnp.concatenate([bd(b_end, n, p), bd(k_end, n, p)], axis=0).T.astype(BF)
        decay_col = jnp.concatenate([part(end_decay, n, p)] * 2, axis=0).T
        st_ref[n, p] = decay_col * st[c] + _dot(bk_t, jnp.concatenate([u_b[c], v_bd[c]], axis=0))

    y = jnp.concatenate(
        [jnp.concatenate([y_bd[n * groups + p][:chunk] + y_bd[n * groups + p][chunk:]
                          for p in range(groups)], axis=1) for n in range(nb)], axis=0)
    rkr = r * k * rk_ref[...]
    outs = []
    for p in range(groups):
        sl = slice(p * LANES, (p + 1) * LANES)
        yp = y[:, sl]
        cen = yp - _dot_lhs2(yp, bdm)
        var = _dot((cen * cen).astype(BF), bdm)
        yn = cen * lax.rsqrt(var + GN_EPS) * lnw_ref[:, sl] + lnb_ref[:, sl]
        bonus = _dot(rkr[:, sl].astype(BF), bdm) * float(HEAD_DIM) * v[:, sl]
        outs.append((yn + bonus) * g[:, sl])
    out = jnp.concatenate(outs, axis=1).astype(o_ref.dtype)
    for n in range(nb):
        o_ref[n] = out[n * chunk:(n + 1) * chunk]


def _rwkv_mix(z, v_first, mu, w0, a0, lora_up, gate_up, k_k, k_a, rk, lnw, lnb, vres, tri, bdm):
    bsz, seq, c_rwkv = z.shape
    d_r = w0.shape[1]
    chunk = RWKV_CHUNK
    nb = RWKV_SEQS
    has_vres = vres is not None
    tok_z = pl.BlockSpec((nb, chunk, c_rwkv), lambda b, c: (b, c, 0))
    tok = pl.BlockSpec((nb, chunk, d_r), lambda b, c: (b, c, 0))
    full = lambda arr: pl.BlockSpec(arr.shape, lambda b, c: (0,) * arr.ndim)
    consts = [mu, w0, a0, lora_up, gate_up, k_k, k_a, rk, lnw, lnb]
    args = [z] + ([v_first] if has_vres else []) + consts + (list(vres) if has_vres else []) + [tri, bdm]
    in_specs = ([tok_z] + ([tok] if has_vres else []) + [full(t) for t in consts]
                + ([full(t) for t in vres] if has_vres else []) + [full(tri), full(bdm)])
    out_specs = [tok] if has_vres else [tok, tok]
    out_shape = [jax.ShapeDtypeStruct((bsz, seq, d_r), BF)]
    if not has_vres:
        out_shape.append(jax.ShapeDtypeStruct((bsz, seq, d_r), F32))
    res = pl.pallas_call(
        functools.partial(_rwkv_kernel, has_vres, d_r),
        grid=(bsz // nb, seq // chunk),
        in_specs=in_specs,
        out_specs=out_specs,
        out_shape=out_shape,
        scratch_shapes=[pltpu.VMEM((nb, d_r // LANES, LANES, LANES), F32),
                        pltpu.VMEM((nb, 8, c_rwkv), F32)],
        compiler_params=_params("arbitrary", "arbitrary"),
        name="rwkv_mix",
    )(*args)
    return (res[0], v_first) if has_vres else (res[0], res[1])


def _sb_kernel(q_ref, k_ref, v_ref, og_ref, tri_ref, bdm_ref, o_ref, acc_ref, carry_ref):
    t = SB_TILE
    qi = pl.program_id(2)
    q = q_ref[0]
    lane = lax.broadcasted_iota(jnp.int32, (t, LANES), 1)
    rowi = lax.broadcasted_iota(jnp.int32, (t, t), 0)
    coli = lax.broadcasted_iota(jnp.int32, (t, t), 1)
    causal = coli < rowi
    zero = jnp.zeros_like(q)
    q_heads = [jnp.where(lane < HEAD_DIM, q, zero), jnp.where(lane >= HEAD_DIM, q, zero)]
    tri = tri_ref[...]

    def scores(j, diagonal):
        start = pl.multiple_of(j * t, t)
        kb = k_ref[0, pl.ds(start, t), :]
        vb = v_ref[0, pl.ds(start, t), :]
        logits = [_dot_nt(q_heads[h], kb) for h in range(2)]
        soft = [_softplus2(s) for s in logits]
        drop = [jnp.where(causal, s, 0.0) for s in soft] if diagonal else soft
        sums = [_dot(d.astype(BF), tri) for d in drop]
        args = [logits[h] - soft[h] - sums[h][:, :t] for h in range(2)]
        return args, [s[:, t:] for s in sums], vb

    def absorb(block, state, diagonal):
        args, totals, vb = block
        out = []
        for h in range(2):
            if diagonal:
                w = jnp.where(causal, jnp.exp2(args[h]), 0.0)
                out.append((_dot(w.astype(BF), vb), totals[h]))
            else:
                acc, carry = state[h]
                w = jnp.exp2(args[h] - jnp.concatenate([carry] * (t // LANES), axis=1))
                out.append((acc + _dot(w.astype(BF), vb), carry + totals[h]))
        return out

    def save(state):
        for h in range(2):
            acc_ref[h], carry_ref[h] = state[h]

    def lightest(state):
        return jnp.minimum(jnp.min(state[0][1]), jnp.min(state[1][1]))

    @pl.when(qi == 0)
    def _():
        save(absorb(scores(qi, True), None, True))

    @pl.when(qi > 0)
    def _():
        newest = scores(qi, True)
        older = scores(qi - 1, False)
        save(absorb(older, absorb(newest, None, True), False))

    def more(state):
        i, mass = state
        return jnp.logical_and(i < qi, mass < SB_DEAD_MASS)

    def body(state):
        i, _ = state
        new = absorb(scores(qi - 1 - i, False), [(acc_ref[h], carry_ref[h]) for h in range(2)], False)
        save(new)
        return i + 1, lightest(new)

    first_mass = jnp.minimum(jnp.min(carry_ref[0]), jnp.min(carry_ref[1]))
    lax.while_loop(more, body, (jnp.int32(1), first_mass))

    o = jnp.where(lane < HEAD_DIM, acc_ref[0], acc_ref[1])
    ms = _dot_lhs2(o * o, bdm_ref[...])
    o_ref[0] = (o * lax.rsqrt(ms + NORM_EPS) * og_ref[...]).astype(o_ref.dtype)


def _sb_attention(q, k, v, og, tri2, bdm):
    bsz, seq, ds = q.shape
    t = SB_TILE
    return pl.pallas_call(
        _sb_kernel,
        grid=(bsz, ds // LANES, seq // t),
        in_specs=[
            pl.BlockSpec((1, t, LANES), lambda b, p, i: (b, i, p)),
            pl.BlockSpec((1, seq, LANES), lambda b, p, i: (b, 0, p)),
            pl.BlockSpec((1, seq, LANES), lambda b, p, i: (b, 0, p)),
            pl.BlockSpec((1, LANES), lambda b, p, i: (0, p)),
            pl.BlockSpec((t, t + LANES), lambda b, p, i: (0, 0)),
            pl.BlockSpec((LANES, LANES), lambda b, p, i: (0, 0)),
        ],
        out_specs=pl.BlockSpec((1, t, LANES), lambda b, p, i: (b, i, p)),
        out_shape=jax.ShapeDtypeStruct((bsz, seq, ds), BF),
        scratch_shapes=[pltpu.VMEM((2, t, LANES), F32), pltpu.VMEM((2, t, LANES), F32)],
        compiler_params=_params("arbitrary", "arbitrary", "arbitrary"),
        name="sb_attention",
    )(q, k, v, og, tri2, bdm)


def _out_proj_kernel(yr_ref, ys_ref, x_ref, mod_ref, g_ref, wr_ref, ws_ref, rwh_ref, rwl_ref, rb_ref,
                     before_ref, xo_ref, h_ref, idx_ref, gate_ref, rank_ref, cnt_ref, seen_ref):
    @pl.when(pl.program_id(0) == 0)
    def _():
        seen_ref[...] = jnp.zeros_like(seen_ref)

    mixed = _dot(yr_ref[...], wr_ref[...]) + _dot(ys_ref[...], ws_ref[...])
    x = x_ref[...] + mod_ref[0, 2:3, :] * mixed
    xo_ref[...] = x
    ms = jnp.mean(x * x, axis=-1, keepdims=True)
    h = x * lax.rsqrt(ms + NORM_EPS) * g_ref[...]
    h = h * (1.0 + mod_ref[0, 4:5, :]) + mod_ref[0, 3:4, :]
    h_ref[...] = _pack_bf16_pairs(h)

    hh, hl = _split2(h)
    rwh = rwh_ref[...]
    logits = _dot_nt(rwh, hh) + (_dot_nt(rwh, hl) + _dot_nt(rwl_ref[...], hh)) + rb_ref[...]
    eidx = lax.broadcasted_iota(jnp.int32, logits.shape, 0)
    tops, idxs, picks = [], [], []
    for _ in range(TOP_K):
        m = jnp.max(logits, axis=0, keepdims=True)
        idx = jnp.min(jnp.where(logits == m, eidx, N_EXPERTS), axis=0, keepdims=True)
        tops.append(m)
        idxs.append(idx)
        picks.append(eidx == idx)
        logits = jnp.where(picks[-1], -jnp.inf, logits)

    one_hot = [p.astype(F32) for p in picks]
    chosen = one_hot[0] + one_hot[1] + one_hot[2] + one_hot[3]
    seen = seen_ref[...]
    ahead = _dot(chosen.astype(BF), before_ref[...]) + seen[:, 0:1]
    ranks = [jnp.sum(o * ahead, axis=0, keepdims=True) for o in one_hot]
    rank_ref[...] = jnp.concatenate(ranks + [jnp.zeros_like(ranks[0])] * (8 - TOP_K), axis=0).astype(jnp.int32)
    seen = seen + jnp.sum(chosen, axis=1, keepdims=True)
    seen_ref[...] = seen
    cnt_ref[...] = seen.astype(jnp.int32)

    exps = [jnp.exp(m - tops[0]) for m in tops]
    den = exps[0] + exps[1] + exps[2] + exps[3]
    pad = jnp.zeros((LANES - TOP_K, den.shape[1]), F32)
    gate_ref[...] = jnp.concatenate([e / den for e in exps] + [pad], axis=0).T
    idx_ref[...] = jnp.concatenate(idxs + [jnp.zeros_like(idxs[0])] * (8 - TOP_K), axis=0)


def _out_proj(yr, ys, x2, mod, g, wr, ws, rwh, rwl, rb, seq):
    n, d = x2.shape
    dr = yr.shape[1]
    ds = ys.shape[1]
    tm = ROW_TILE
    per_seq = seq // tm
    row = lambda i: (i, 0)
    const = lambda i: (0, 0)
    colblk = lambda i: (0, i)
    ti = jnp.arange(tm)
    before = (ti[:, None] < ti[None, :]).astype(BF)
    return pl.pallas_call(
        _out_proj_kernel,
        grid=(n // tm,),
        in_specs=[
            pl.BlockSpec((tm, dr), row),
            pl.BlockSpec((tm, ds), row),
            pl.BlockSpec((tm, d), row),
            pl.BlockSpec((1,) + mod.shape[1:], lambda i: (i // per_seq, 0, 0)),
            pl.BlockSpec((1, d), const),
            pl.BlockSpec((dr, d), const),
            pl.BlockSpec((ds, d), const),
            pl.BlockSpec((N_EXPERTS, d), const),
            pl.BlockSpec((N_EXPERTS, d), const),
            pl.BlockSpec((N_EXPERTS, 1), const),
            pl.BlockSpec((tm, tm), const),
        ],
        out_specs=[
            pl.BlockSpec((tm, d), row),
            pl.BlockSpec((tm, d // 2), row),
            pl.BlockSpec((8, tm), colblk),
            pl.BlockSpec((tm, LANES), row),
            pl.BlockSpec((8, tm), colblk),
            pl.BlockSpec((N_EXPERTS, LANES), const),
        ],
        out_shape=[
            jax.ShapeDtypeStruct((n, d), F32),
            jax.ShapeDtypeStruct((n, d // 2), jnp.uint32),
            jax.ShapeDtypeStruct((8, n), jnp.int32),
            jax.ShapeDtypeStruct((n, LANES), F32),
            jax.ShapeDtypeStruct((8, n), jnp.int32),
            jax.ShapeDtypeStruct((N_EXPERTS, LANES), jnp.int32),
        ],
        scratch_shapes=[pltpu.VMEM((N_EXPERTS, LANES), F32)],
        compiler_params=_params("arbitrary"),
        name="out_proj_router",
    )(yr, ys, x2, mod, g, wr, ws, rwh, rwl, rb, before)


def _moe_kernel(te_ref, nu_ref, x_ref, w1_ref, b1_ref, w2_ref, b2_ref, o_ref, w1b_ref, w2b_ref):
    i = pl.program_id(0)
    f = w2_ref.shape[2]
    live = i < nu_ref[0]
    new_expert = jnp.logical_or(i == 0, te_ref[i] != te_ref[jnp.maximum(i - 1, 0)])

    @pl.when(jnp.logical_and(live, new_expert))
    def _():
        w1b_ref[...] = w1_ref[0, 0].astype(BF)
        w2b_ref[...] = w2_ref[0, 0].astype(BF)

    @pl.when(live)
    def _():
        lo, hi = _unpack_bf16_pairs(x_ref[...])
        x = jnp.concatenate([lo.astype(BF), hi.astype(BF)], axis=1)
        hid = _dot(x, w1b_ref[...]) + b1_ref[0, 0]
        glu = jnp.minimum(hid[:, :f], SWIGLU_LIMIT)
        lin = jnp.clip(hid[:, f:], -SWIGLU_LIMIT, SWIGLU_LIMIT)
        act = glu * jax.nn.sigmoid(SWIGLU_ALPHA * glu) * (lin + 1.0)
        o_ref[...] = _pack_bf16_pairs(_dot(act.astype(BF), w2b_ref[...]) + b2_ref[0, 0])

    @pl.when(jnp.logical_not(live))
    def _():
        o_ref[...] = jnp.zeros_like(o_ref)


def _moe_experts(layer, tile_expert, n_used, xs, w1, b1, w2, b2):
    n_rows, half = xs.shape
    _, _, d, f2 = w1.shape
    f = w2.shape[2]
    tm = MOE_TILE
    expert = lambda i, te, nu: (layer, te[i], 0, 0)
    return pl.pallas_call(
        _moe_kernel,
        grid_spec=pltpu.PrefetchScalarGridSpec(
            num_scalar_prefetch=2,
            grid=(n_rows // tm,),
            in_specs=[
                pl.BlockSpec((tm, half), lambda i, te, nu: (i, 0)),
                pl.BlockSpec((1, 1, d, f2), expert),
                pl.BlockSpec((1, 1, 1, f2), expert),
                pl.BlockSpec((1, 1, f, d), expert),
                pl.BlockSpec((1, 1, 1, d), expert),
            ],
            out_specs=pl.BlockSpec((tm, half), lambda i, te, nu: (i, 0)),
            scratch_shapes=[pltpu.VMEM((d, f2), BF), pltpu.VMEM((f, d), BF)],
        ),
        out_shape=jax.ShapeDtypeStruct((n_rows, half), jnp.uint32),
        compiler_params=pltpu.CompilerParams(dimension_semantics=("arbitrary",),
                                             vmem_limit_bytes=MOE_VMEM_LIMIT),
        name="moe_experts",
    )(tile_expert, n_used, xs, w1, b1.reshape(b1.shape[:2] + (1, f2)), w2,
      b2.reshape(b2.shape[:2] + (1, d)))


def _combine_kernel(x_ref, mod_ref, gate_ref, y0_ref, y1_ref, y2_ref, y3_ref, o_ref):
    half = y0_ref.shape[1]
    lo = hi = None
    for s, y_ref in enumerate((y0_ref, y1_ref, y2_ref, y3_ref)):
        gate = gate_ref[:, s:s + 1]
        l2, h2 = _unpack_bf16_pairs(y_ref[...])
        lo = gate * l2 if lo is None else lo + gate * l2
        hi = gate * h2 if hi is None else hi + gate * h2
    o_ref[:, :half] = x_ref[:, :half] + mod_ref[0, 5:6, :half] * lo
    o_ref[:, half:] = x_ref[:, half:] + mod_ref[0, 5:6, half:] * hi


def _combine(x_mid, mod, gate_rows, ys_by_slot, seq):
    n, d = x_mid.shape
    tm = ROW_TILE
    per_seq = seq // tm
    row = lambda i: (i, 0)
    return pl.pallas_call(
        _combine_kernel,
        grid=(n // tm,),
        in_specs=[pl.BlockSpec((tm, d), row),
                  pl.BlockSpec((1,) + mod.shape[1:], lambda i: (i // per_seq, 0, 0)),
                  pl.BlockSpec((tm, LANES), row)]
                 + [pl.BlockSpec((tm, d // 2), row)] * TOP_K,
        out_specs=pl.BlockSpec((tm, d), row),
        out_shape=jax.ShapeDtypeStruct((n, d), F32),
        compiler_params=_params("arbitrary"),
        name="moe_combine",
    )(x_mid, mod, gate_rows, *ys_by_slot)


def _moe_ffn(layer, h_packed, top_idx, rank, counts, w1, b1, w2, b2):
    n_tok = h_packed.shape[0]
    tm = MOE_TILE
    n_assign = n_tok * TOP_K
    n_rows = n_assign + N_EXPERTS * tm
    n_tiles = n_rows // tm
    flat_e = top_idx.T.reshape(-1)
    order = jnp.argsort(flat_e, stable=True)
    padded = (counts + tm - 1) // tm * tm
    padded_end = jnp.cumsum(padded)
    padded_start = padded_end - padded
    start = jnp.cumsum(counts) - counts
    pos = padded_start[top_idx] + rank
    tile_start = jnp.arange(n_tiles, dtype=jnp.int32) * tm
    tile_expert = jnp.minimum(
        jnp.sum((tile_start[:, None] >= padded_end[None, :]).astype(jnp.int32), axis=1),
        N_EXPERTS - 1).astype(jnp.int32)
    n_used = (padded_end[-1:] // tm).astype(jnp.int32)
    within = jnp.arange(tm, dtype=jnp.int32)[None, :]
    local = (tile_start - padded_start[tile_expert])[:, None] + within
    src = jnp.clip(start[tile_expert][:, None] + local, 0, n_assign - 1)
    valid = local < counts[tile_expert][:, None]
    row_tok = jnp.where(valid, order[src] // TOP_K, (tile_start[:, None] + within) % n_tok)
    ys = _moe_experts(layer, tile_expert, n_used, h_packed[row_tok.reshape(-1)], w1, b1, w2, b2)
    return [ys[pos[s]] for s in range(TOP_K)]


def _block_diag_mean():
    i = jnp.arange(LANES)
    return jnp.where((i[:, None] // HEAD_DIM) == (i[None, :] // HEAD_DIM), 1.0 / HEAD_DIM, 0.0).astype(BF)


def _block_diag2(a, b):
    top = jnp.concatenate([a, jnp.zeros((a.shape[0], b.shape[1]), a.dtype)], axis=1)
    bot = jnp.concatenate([jnp.zeros((b.shape[0], a.shape[1]), b.dtype), b], axis=1)
    return jnp.concatenate([top, bot], axis=0)


@jax.jit
def _forward(x, c, norm1_g, norm2_g, ada_w, ada_b, w_in, shift_mu, decay_w0, decay_up, iclr_a0,
             iclr_up, gate_up, k_k, k_a, r_k, lnx_w, lnx_b, vres_down, vres_up, vres_b, q_norm_g,
             k_norm_g, sb_out_g, w_out, router_w, router_b, exp_w1, exp_b1, exp_w2, exp_b2):
    bsz, seq, d = x.shape
    depth = w_in.shape[0]
    d_r = decay_w0.shape[1]
    c_rwkv = shift_mu.shape[1]
    d_sb = sb_out_g.shape[1]
    n_tok = bsz * seq
    assert seq % ROW_TILE == 0 and seq % SB_TILE == 0 and seq % RWKV_CHUNK == 0
    assert bsz % RWKV_SEQS == 0 and DECAY_LORA + ICLR_LORA == LANES and GATE_LORA == LANES

    bdm = _block_diag_mean()
    ci = jnp.arange(RWKV_CHUNK)
    tri_chunk = (ci[None, :] <= ci[:, None]).astype(BF)
    si = jnp.arange(SB_TILE)
    tri_sb = jnp.concatenate([(si[:, None] > si[None, :]).astype(BF),
                              jnp.ones((SB_TILE, LANES), BF)], axis=1)
    row_vec = lambda t: t.reshape(1, -1)

    mod_all = _ada_mod(c, ada_w, ada_b)
    x2 = x.reshape(n_tok, d)
    v_first = None
    for l in range(depth):
        mod = mod_all[l]
        zr, q, k, v = _in_proj(
            x2, mod, row_vec(norm1_g[l]), w_in[l].astype(BF), bdm,
            row_vec(jnp.tile(q_norm_g[l], LANES // HEAD_DIM)),
            row_vec(jnp.tile(k_norm_g[l], LANES // HEAD_DIM)), seq, c_rwkv, d_sb)
        if l == 0:
            vres = None
        else:
            rank = vres_down.shape[2]
            vres = (jnp.pad(vres_down[l - 1], ((0, 0), (0, LANES - rank))).astype(BF),
                    jnp.pad(vres_up[l - 1], ((0, LANES - rank), (0, 0))).astype(BF),
                    row_vec(vres_b[l - 1]))
        y_r, v_first = _rwkv_mix(
            zr.reshape(bsz, seq, c_rwkv), v_first, row_vec(shift_mu[l]), row_vec(decay_w0[l]),
            row_vec(iclr_a0[l]), _block_diag2(decay_up[l], iclr_up[l]).astype(BF),
            gate_up[l].astype(BF), row_vec(k_k[l]), row_vec(k_a[l]), row_vec(r_k[l]),
            row_vec(lnx_w[l]), row_vec(lnx_b[l]), vres, tri_chunk, bdm)
        y_s = _sb_attention(q.reshape(bsz, seq, d_sb), k.reshape(bsz, seq, d_sb),
                            v.reshape(bsz, seq, d_sb), row_vec(sb_out_g[l]), tri_sb, bdm)
        rw_t = router_w[l].T
        rwh = rw_t.astype(BF)
        rwl = (rw_t - rwh.astype(F32)).astype(BF)
        w_o = w_out[l].astype(BF)
        x_mid, h2, top_idx, gate_rows, rank, seen = _out_proj(
            y_r.reshape(n_tok, d_r), y_s.reshape(n_tok, d_sb), x2, mod, row_vec(norm2_g[l]),
            w_o[:d_r], w_o[d_r:], rwh, rwl, router_b[l].reshape(N_EXPERTS, 1), seq)
        ys_by_slot = _moe_ffn(l, h2, top_idx[:TOP_K], rank[:TOP_K], seen[:, 0], exp_w1, exp_b1,
                              exp_w2, exp_b2)
        x2 = _combine(x_mid, mod, gate_rows, ys_by_slot, seq)
    return x2.reshape(bsz, seq, d)


def kernel(x, c, norm1_g, norm2_g, ada_w, ada_b, w_in, shift_mu, decay_w0, decay_up, iclr_a0, iclr_up,
           gate_up, k_k, k_a, r_k, lnx_w, lnx_b, vres_down, vres_up, vres_b, q_norm_g, k_norm_g,
           sb_out_g, w_out, router_w, router_b, exp_w1, exp_b1, exp_w2, exp_b2):
    return _forward(x, c, norm1_g, norm2_g, ada_w, ada_b, w_in, shift_mu, decay_w0, decay_up, iclr_a0,
                    iclr_up, gate_up, k_k, k_a, r_k, lnx_w, lnx_b, vres_down, vres_up, vres_b,
                    q_norm_g, k_norm_g, sb_out_g, w_out, router_w, router_b, exp_w1, exp_b1,
                    exp_w2, exp_b2)
```

```python
import functools
import math

import jax
import jax.numpy as jnp
from jax import lax
from jax.experimental import pallas as pl
from jax.experimental.pallas import tpu as pltpu

F32 = jnp.float32
BF = jnp.bfloat16

HEAD_DIM = 64
LANES = 128
RWKV_CHUNK = 64
RWKV_SEQS = 2
DECAY_LORA = 64
ICLR_LORA = 64
GATE_LORA = 128
N_EXPERTS = 32
TOP_K = 4
SWIGLU_ALPHA = 1.702
SWIGLU_LIMIT = 7.0
NORM_EPS = 1e-6
GN_EPS = 1e-5 * HEAD_DIM
L2_EPS = 1e-12

ROW_TILE = 512
SB_TILE = 256
MOE_TILE = 512
VMEM_LIMIT = 48 * 1024 * 1024
MOE_VMEM_LIMIT = 58 * 1024 * 1024
LOG2E = 1.4426950408889634
SB_DEAD_MASS = 150.0


def _dot(a, b):
    return jnp.dot(a, b, preferred_element_type=F32)


def _dot_nt(a, b):
    return lax.dot_general(a, b, (((1,), (1,)), ((), ())), preferred_element_type=F32)


def _split2(x):
    hi = x.astype(BF)
    lo = (x - hi.astype(F32)).astype(BF)
    return hi, lo


def _dot_lhs2(x, m_bf):
    hi, lo = _split2(x)
    return _dot(hi, m_bf) + _dot(lo, m_bf)


def _dot3(a, b):
    ah, al = _split2(a)
    bh, bl = _split2(b)
    return _dot(ah, bh) + (_dot(ah, bl) + _dot(al, bh))


def _softplus(x):
    return jnp.maximum(x, 0.0) + jnp.log(1.0 + jnp.exp(-jnp.abs(x)))


def _softplus2(x):
    return jnp.maximum(x, 0.0) + jnp.log2(1.0 + jnp.exp2(-jnp.abs(x)))


def _params(*sem):
    return pltpu.CompilerParams(dimension_semantics=sem, vmem_limit_bytes=VMEM_LIMIT)


def _ada_kernel(c_ref, w_ref, b_ref, o_ref):
    c = c_ref[...]
    cond = c * jax.nn.sigmoid(c)
    o_ref[0] = _dot3(cond, w_ref[0]) + b_ref[0]


def _ada_mod(c, ada_w, ada_b):
    depth, d, d6 = ada_w.shape
    bsz = c.shape[0]
    out = pl.pallas_call(
        _ada_kernel,
        grid=(depth, d6 // d),
        in_specs=[
            pl.BlockSpec((bsz, d), lambda l, j: (0, 0)),
            pl.BlockSpec((1, d, d), lambda l, j: (l, 0, j)),
            pl.BlockSpec((1, 1, d), lambda l, j: (l, 0, j)),
        ],
        out_specs=pl.BlockSpec((1, bsz, d), lambda l, j: (l, 0, j)),
        out_shape=jax.ShapeDtypeStruct((depth, bsz, d6), F32),
        compiler_params=_params("arbitrary", "arbitrary"),
        name="ada_mod",
    )(c, ada_w, ada_b.reshape(depth, 1, d6))
    return out.reshape(depth, bsz, d6 // d, d)


def _in_proj_kernel(c_rwkv, d_sb, x_ref, mod_ref, g_ref, w_ref, bdm_ref, qg_ref, kg_ref,
                    zr_ref, q_ref, k_ref, v_ref):
    x = x_ref[...]
    ms = jnp.mean(x * x, axis=-1, keepdims=True)
    h = x * lax.rsqrt(ms + NORM_EPS) * g_ref[...]
    h = h * (1.0 + mod_ref[0, 1:2, :]) + mod_ref[0, 0:1, :]
    z = _dot(h.astype(BF), w_ref[...])
    zr_ref[...] = z[:, :c_rwkv]
    bdm = bdm_ref[...]
    scale = LOG2E / math.sqrt(HEAD_DIM)
    for p in range(d_sb // LANES):
        lo = p * LANES
        q = z[:, c_rwkv + lo:c_rwkv + lo + LANES]
        k = z[:, c_rwkv + d_sb + lo:c_rwkv + d_sb + lo + LANES]
        qn = q * lax.rsqrt(_dot_lhs2(q * q, bdm) + NORM_EPS) * qg_ref[...]
        kn = k * lax.rsqrt(_dot_lhs2(k * k, bdm) + NORM_EPS) * kg_ref[...]
        q_ref[:, lo:lo + LANES] = (qn * scale).astype(BF)
        k_ref[:, lo:lo + LANES] = kn.astype(BF)
    v_ref[...] = z[:, c_rwkv + 2 * d_sb:].astype(BF)


def _in_proj(x2, mod, g, w_bf, bdm, qg, kg, seq, c_rwkv, d_sb):
    n, d = x2.shape
    c_in = w_bf.shape[1]
    tm = ROW_TILE
    per_seq = seq // tm
    row = lambda i: (i, 0)
    const = lambda i: (0, 0)
    return pl.pallas_call(
        functools.partial(_in_proj_kernel, c_rwkv, d_sb),
        grid=(n // tm,),
        in_specs=[
            pl.BlockSpec((tm, d), row),
            pl.BlockSpec((1,) + mod.shape[1:], lambda i: (i // per_seq, 0, 0)),
            pl.BlockSpec((1, d), const),
            pl.BlockSpec((d, c_in), const),
            pl.BlockSpec((LANES, LANES), const),
            pl.BlockSpec((1, LANES), const),
            pl.BlockSpec((1, LANES), const),
        ],
        out_specs=[
            pl.BlockSpec((tm, c_rwkv), row),
            pl.BlockSpec((tm, d_sb), row),
            pl.BlockSpec((tm, d_sb), row),
            pl.BlockSpec((tm, d_sb), row),
        ],
        out_shape=[
            jax.ShapeDtypeStruct((n, c_rwkv), F32),
            jax.ShapeDtypeStruct((n, d_sb), BF),
            jax.ShapeDtypeStruct((n, d_sb), BF),
            jax.ShapeDtypeStruct((n, d_sb), BF),
        ],
        compiler_params=_params("arbitrary"),
        name="in_proj",
    )(x2, mod, g, w_bf, bdm, qg, kg)


def _rwkv_kernel(has_vres, d_r, *refs):
    if has_vres:
        (z_ref, vf_ref, mu_ref, w0_ref, a0_ref, lora_ref, gup_ref, kk_ref, ka_ref, rk_ref, lnw_ref,
         lnb_ref, vd_ref, vu_ref, vb_ref, tri_ref, bdm_ref, o_ref, st_ref, prev_ref) = refs
    else:
        (z_ref, mu_ref, w0_ref, a0_ref, lora_ref, gup_ref, kk_ref, ka_ref, rk_ref, lnw_ref,
         lnb_ref, tri_ref, bdm_ref, o_ref, vo_ref, st_ref, prev_ref) = refs
    chunk = RWKV_CHUNK
    two = 2 * chunk
    nb = z_ref.shape[0]
    groups = d_r // LANES

    @pl.when(pl.program_id(1) == 0)
    def _():
        st_ref[...] = jnp.zeros_like(st_ref)
        prev_ref[...] = jnp.zeros_like(prev_ref)

    first = lax.broadcasted_iota(jnp.int32, (chunk, z_ref.shape[2]), 0) == 0
    mu = mu_ref[...]
    zs_parts = []
    for n in range(nb):
        z = z_ref[n]
        z_prev = jnp.where(first, prev_ref[n, 0:1, :], pltpu.roll(z, 1, axis=0))
        prev_ref[n, 0:1, :] = z[chunk - 1:chunk, :]
        zs_parts.append(z + (z_prev - z) * mu)
    zs = jnp.concatenate(zs_parts, axis=0)

    r = zs[:, :d_r]
    k = zs[:, d_r:2 * d_r]
    v = zs[:, 2 * d_r:3 * d_r]
    wa = zs[:, 3 * d_r:3 * d_r + LANES]
    g_lo = zs[:, 3 * d_r + LANES:]
    lane = lax.broadcasted_iota(jnp.int32, wa.shape, 1)
    feed = jnp.where(lane < DECAY_LORA, jnp.tanh(wa), wa)
    lora = _dot(feed.astype(BF), lora_ref[...])
    w_pre = w0_ref[...] + lora[:, :d_r]
    lw = -jnp.exp(-_softplus(-w_pre) - 0.5)
    a = jax.nn.sigmoid(a0_ref[...] + lora[:, d_r:])
    g = _dot(jax.nn.sigmoid(g_lo).astype(BF), gup_ref[...])
    if has_vres:
        vf = jnp.concatenate([vf_ref[n] for n in range(nb)], axis=0)
        low = _dot(v.astype(BF), vd_ref[...])
        v = v + (vf - v) * jax.nn.sigmoid(vb_ref[...] + _dot(low.astype(BF), vu_ref[...]))
    else:
        for n in range(nb):
            vo_ref[n] = v[n * chunk:(n + 1) * chunk]

    bdm = bdm_ref[...]
    kk = k * kk_ref[...]
    ssq = jnp.concatenate(
        [_dot(jnp.square(kk[:, i * LANES:(i + 1) * LANES]).astype(BF), bdm) for i in range(groups)],
        axis=1) * float(HEAD_DIM)
    kk = kk / jnp.maximum(jnp.sqrt(ssq), L2_EPS)
    k = k * (1.0 + (a - 1.0) * ka_ref[...])

    tri = tri_ref[...]
    l1 = lw.astype(BF)
    rem = lw - l1.astype(F32)
    l2 = rem.astype(BF)
    l3 = (rem - l2.astype(F32)).astype(BF)
    cum_parts, total_parts = [], []
    for n in range(nb):
        rows = slice(n * chunk, (n + 1) * chunk)
        c = _dot(tri, l1[rows]) + (_dot(tri, l2[rows]) + _dot(tri, l3[rows]))
        cum_parts.append(c)
        total_parts.append(jnp.broadcast_to(c[chunk - 1:chunk, :], c.shape))
    cum = jnp.concatenate(cum_parts, axis=0)
    total = jnp.concatenate(total_parts, axis=0)
    grow = jnp.exp(-cum)
    to_end = jnp.exp(total - cum)
    b = kk * a
    a_src = -(kk * jnp.exp(cum - lw))
    r_src = r * jnp.exp(cum)
    b_src = b * grow
    k_src = k * grow
    b_end = b * to_end
    k_end = k * to_end
    end_decay = jnp.exp(total)

    row = lax.broadcasted_iota(jnp.int32, (two, LANES), 0)
    col = lax.broadcasted_iota(jnp.int32, (two, LANES), 1)
    same_head = (row < chunk) == (col < HEAD_DIM)
    row_in = row & (chunk - 1)
    col_in = col & (chunk - 1)
    strict = row_in > col_in
    incl = row_in >= col_in
    eye = (row == col).astype(F32)

    chains = [(n, p) for n in range(nb) for p in range(groups)]

    def part(t, n, p):
        return t[n * chunk:(n + 1) * chunk, p * LANES:(p + 1) * LANES]

    def bd(t, n, p):
        s = part(t, n, p)
        return jnp.where(same_head, jnp.concatenate([s, s], axis=0), 0.0)

    a_s = [bd(a_src, n, p).astype(BF) for n, p in chains]
    r_s = [bd(r_src, n, p).astype(BF) for n, p in chains]
    b_s = [bd(b_src, n, p).astype(BF) for n, p in chains]
    k_s = [bd(k_src, n, p).astype(BF) for n, p in chains]
    v_bd = [bd(v, n, p).astype(BF) for n, p in chains]
    st = [st_ref[n, p] for n, p in chains]
    st_b = [s.astype(BF) for s in st]

    pair = [_dot_nt(jnp.concatenate([a_s[c], r_s[c]], axis=0),
                    jnp.concatenate([b_s[c], k_s[c]], axis=0)) for c in range(len(chains))]
    n_ab = [jnp.where(strict, m[:two, :two], 0.0) for m in pair]
    a_ak = [jnp.where(strict, m[:two, two:], 0.0).astype(BF) for m in pair]
    a_rb = [jnp.where(incl, m[two:, :two], 0.0).astype(BF) for m in pair]
    a_rk = [jnp.where(incl, m[two:, two:], 0.0).astype(BF) for m in pair]

    inv = [eye + m for m in n_ab]
    power = [m.astype(BF) for m in n_ab]
    for _ in range(int(math.log2(chunk)) - 1):
        power = [_dot(m, m).astype(BF) for m in power]
        inv = [i + _dot(i.astype(BF), m) for i, m in zip(inv, power)]

    x0 = [_dot(jnp.concatenate([a_s[c], a_ak[c]], axis=1),
               jnp.concatenate([st_b[c], v_bd[c]], axis=0)) for c in range(len(chains))]
    u_b = [_dot(inv[c].astype(BF), x0[c].astype(BF)).astype(BF) for c in range(len(chains))]
    y_bd = [_dot(jnp.concatenate([r_s[c], a_rb[c], a_rk[c]], axis=1),
                 jnp.concatenate([st_b[c], u_b[c], v_bd[c]], axis=0)) for c in range(len(chains))]
    for c, (n, p) in enumerate(chains):
        bk_t = jnp.concatenate([bd(b_end, n, p), bd(k_end, n, p)], axis=0).T.astype(BF)
        decay_col = jnp.concatenate([part(end_decay, n, p)] * 2, axis=0).T
        st_ref[n, p] = decay_col * st[c] + _dot(bk_t, jnp.concatenate([u_b[c], v_bd[c]], axis=0))

    y = jnp.concatenate(
        [jnp.concatenate([y_bd[n * groups + p][:chunk] + y_bd[n * groups + p][chunk:]
                          for p in range(groups)], axis=1) for n in range(nb)], axis=0)
    rkr = r * k * rk_ref[...]
    outs = []
    for p in range(groups):
        sl = slice(p * LANES, (p + 1) * LANES)
        yp = y[:, sl]
        cen = yp - _dot_lhs2(yp, bdm)
        var = _dot((cen * cen).astype(BF), bdm)
        yn = cen * lax.rsqrt(var + GN_EPS) * lnw_ref[:, sl] + lnb_ref[:, sl]
        bonus = _dot(rkr[:, sl].astype(BF), bdm) * float(HEAD_DIM) * v[:, sl]
        outs.append((yn + bonus) * g[:, sl])
    out = jnp.concatenate(outs, axis=1).astype(o_ref.dtype)
    for n in range(nb):
        o_ref[n] = out[n * chunk:(n + 1) * chunk]


def _rwkv_mix(z, v_first, mu, w0, a0, lora_up, gate_up, k_k, k_a, rk, lnw, lnb, vres, tri, bdm):
    bsz, seq, c_rwkv = z.shape
    d_r = w0.shape[1]
    chunk = RWKV_CHUNK
    nb = RWKV_SEQS
    has_vres = vres is not None
    tok_z = pl.BlockSpec((nb, chunk, c_rwkv), lambda b, c: (b, c, 0))
    tok = pl.BlockSpec((nb, chunk, d_r), lambda b, c: (b, c, 0))
    full = lambda arr: pl.BlockSpec(arr.shape, lambda b, c: (0,) * arr.ndim)
    consts = [mu, w0, a0, lora_up, gate_up, k_k, k_a, rk, lnw, lnb]
    args = [z] + ([v_first] if has_vres else []) + consts + (list(vres) if has_vres else []) + [tri, bdm]
    in_specs = ([tok_z] + ([tok] if has_vres else []) + [full(t) for t in consts]
                + ([full(t) for t in vres] if has_vres else []) + [full(tri), full(bdm)])
    out_specs = [tok] if has_vres else [tok, tok]
    out_shape = [jax.ShapeDtypeStruct((bsz, seq, d_r), BF)]
    if not has_vres:
        out_shape.append(jax.ShapeDtypeStruct((bsz, seq, d_r), F32))
    res = pl.pallas_call(
        functools.partial(_rwkv_kernel, has_vres, d_r),
        grid=(bsz // nb, seq // chunk),
        in_specs=in_specs,
        out_specs=out_specs,
        out_shape=out_shape,
        scratch_shapes=[pltpu.VMEM((nb, d_r // LANES, LANES, LANES), F32),
                        pltpu.VMEM((nb, 8, c_rwkv), F32)],
        compiler_params=_params("arbitrary", "arbitrary"),
        name="rwkv_mix",
    )(*args)
    return (res[0], v_first) if has_vres else (res[0], res[1])


def _sb_kernel(q_ref, k_ref, v_ref, og_ref, tri_ref, bdm_ref, o_ref, acc_ref, carry_ref):
    t = SB_TILE
    qi = pl.program_id(2)
    q = q_ref[0]
    lane = lax.broadcasted_iota(jnp.int32, (t, LANES), 1)
    rowi = lax.broadcasted_iota(jnp.int32, (t, t), 0)
    coli = lax.broadcasted_iota(jnp.int32, (t, t), 1)
    causal = coli < rowi
    zero = jnp.zeros_like(q)
    q_heads = [jnp.where(lane < HEAD_DIM, q, zero), jnp.where(lane >= HEAD_DIM, q, zero)]
    tri = tri_ref[...]

    def scores(j, diagonal):
        start = pl.multiple_of(j * t, t)
        kb = k_ref[0, pl.ds(start, t), :]
        vb = v_ref[0, pl.ds(start, t), :]
        logits = [_dot_nt(q_heads[h], kb) for h in range(2)]
        soft = [_softplus2(s) for s in logits]
        drop = [jnp.where(causal, s, 0.0) for s in soft] if diagonal else soft
        sums = [_dot(d.astype(BF), tri) for d in drop]
        args = [logits[h] - soft[h] - sums[h][:, :t] for h in range(2)]
        return args, [s[:, t:] for s in sums], vb

    def absorb(block, state, diagonal):
        args, totals, vb = block
        out = []
        for h in range(2):
            if diagonal:
                w = jnp.where(causal, jnp.exp2(args[h]), 0.0)
                out.append((_dot(w.astype(BF), vb), totals[h]))
            else:
                acc, carry = state[h]
                w = jnp.exp2(args[h] - jnp.concatenate([carry] * (t // LANES), axis=1))
                out.append((acc + _dot(w.astype(BF), vb), carry + totals[h]))
        return out

    def save(state):
        for h in range(2):
            acc_ref[h], carry_ref[h] = state[h]

    def lightest(state):
        return jnp.minimum(jnp.min(state[0][1]), jnp.min(state[1][1]))

    @pl.when(qi == 0)
    def _():
        save(absorb(scores(qi, True), None, True))

    @pl.when(qi > 0)
    def _():
        newest = scores(qi, True)
        older = scores(qi - 1, False)
        save(absorb(older, absorb(newest, None, True), False))

    def more(state):
        i, mass = state
        return jnp.logical_and(i < qi, mass < SB_DEAD_MASS)

    def body(state):
        i, _ = state
        new = absorb(scores(qi - 1 - i, False), [(acc_ref[h], carry_ref[h]) for h in range(2)], False)
        save(new)
        return i + 1, lightest(new)

    first_mass = jnp.minimum(jnp.min(carry_ref[0]), jnp.min(carry_ref[1]))
    lax.while_loop(more, body, (jnp.int32(1), first_mass))

    o = jnp.where(lane < HEAD_DIM, acc_ref[0], acc_ref[1])
    ms = _dot_lhs2(o * o, bdm_ref[...])
    o_ref[0] = (o * lax.rsqrt(ms + NORM_EPS) * og_ref[...]).astype(o_ref.dtype)


def _sb_attention(q, k, v, og, tri2, bdm):
    bsz, seq, ds = q.shape
    t = SB_TILE
    return pl.pallas_call(
        _sb_kernel,
        grid=(bsz, ds // LANES, seq // t),
        in_specs=[
            pl.BlockSpec((1, t, LANES), lambda b, p, i: (b, i, p)),
            pl.BlockSpec((1, seq, LANES), lambda b, p, i: (b, 0, p)),
            pl.BlockSpec((1, seq, LANES), lambda b, p, i: (b, 0, p)),
            pl.BlockSpec((1, LANES), lambda b, p, i: (0, p)),
            pl.BlockSpec((t, t + LANES), lambda b, p, i: (0, 0)),
            pl.BlockSpec((LANES, LANES), lambda b, p, i: (0, 0)),
        ],
        out_specs=pl.BlockSpec((1, t, LANES), lambda b, p, i: (b, i, p)),
        out_shape=jax.ShapeDtypeStruct((bsz, seq, ds), BF),
        scratch_shapes=[pltpu.VMEM((2, t, LANES), F32), pltpu.VMEM((2, t, LANES), F32)],
        compiler_params=_params("arbitrary", "arbitrary", "arbitrary"),
        name="sb_attention",
    )(q, k, v, og, tri2, bdm)


def _out_proj_kernel(yr_ref, ys_ref, x_ref, mod_ref, g_ref, wr_ref, ws_ref, rwh_ref, rwl_ref, rb_ref,
                     before_ref, xo_ref, h_ref, idx_ref, gate_ref, rank_ref, cnt_ref, seen_ref):
    @pl.when(pl.program_id(0) == 0)
    def _():
        seen_ref[...] = jnp.zeros_like(seen_ref)

    mixed = _dot(yr_ref[...], wr_ref[...]) + _dot(ys_ref[...], ws_ref[...])
    x = x_ref[...] + mod_ref[0, 2:3, :] * mixed
    xo_ref[...] = x
    ms = jnp.mean(x * x, axis=-1, keepdims=True)
    h = x * lax.rsqrt(ms + NORM_EPS) * g_ref[...]
    h = h * (1.0 + mod_ref[0, 4:5, :]) + mod_ref[0, 3:4, :]
    h_ref[...] = h

    hh, hl = _split2(h)
    rwh = rwh_ref[...]
    logits = _dot_nt(rwh, hh) + (_dot_nt(rwh, hl) + _dot_nt(rwl_ref[...], hh)) + rb_ref[...]
    eidx = lax.broadcasted_iota(jnp.int32, logits.shape, 0)
    tops, idxs, picks = [], [], []
    for _ in range(TOP_K):
        m = jnp.max(logits, axis=0, keepdims=True)
        idx = jnp.min(jnp.where(logits == m, eidx, N_EXPERTS), axis=0, keepdims=True)
        tops.append(m)
        idxs.append(idx)
        picks.append(eidx == idx)
        logits = jnp.where(picks[-1], -jnp.inf, logits)

    one_hot = [p.astype(F32) for p in picks]
    chosen = one_hot[0] + one_hot[1] + one_hot[2] + one_hot[3]
    seen = seen_ref[...]
    ahead = _dot(chosen.astype(BF), before_ref[...]) + seen[:, 0:1]
    ranks = [jnp.sum(o * ahead, axis=0, keepdims=True) for o in one_hot]
    rank_ref[...] = jnp.concatenate(ranks + [jnp.zeros_like(ranks[0])] * (8 - TOP_K), axis=0).astype(jnp.int32)
    seen = seen + jnp.sum(chosen, axis=1, keepdims=True)
    seen_ref[...] = seen
    cnt_ref[...] = seen.astype(jnp.int32)

    exps = [jnp.exp(m - tops[0]) for m in tops]
    den = exps[0] + exps[1] + exps[2] + exps[3]
    pad = jnp.zeros((LANES - TOP_K, den.shape[1]), F32)
    gate_ref[...] = jnp.concatenate([e / den for e in exps] + [pad], axis=0).T
    idx_ref[...] = jnp.concatenate(idxs + [jnp.zeros_like(idxs[0])] * (8 - TOP_K), axis=0)


def _out_proj(yr, ys, x2, mod, g, wr, ws, rwh, rwl, rb, seq):
    n, d = x2.shape
    dr = yr.shape[1]
    ds = ys.shape[1]
    tm = ROW_TILE
    per_seq = seq // tm
    row = lambda i: (i, 0)
    const = lambda i: (0, 0)
    colblk = lambda i: (0, i)
    ti = jnp.arange(tm)
    before = (ti[:, None] < ti[None, :]).astype(BF)
    return pl.pallas_call(
        _out_proj_kernel,
        grid=(n // tm,),
        in_specs=[
            pl.BlockSpec((tm, dr), row),
            pl.BlockSpec((tm, ds), row),
            pl.BlockSpec((tm, d), row),
            pl.BlockSpec((1,) + mod.shape[1:], lambda i: (i // per_seq, 0, 0)),
            pl.BlockSpec((1, d), const),
            pl.BlockSpec((dr, d), const),
            pl.BlockSpec((ds, d), const),
            pl.BlockSpec((N_EXPERTS, d), const),
            pl.BlockSpec((N_EXPERTS, d), const),
            pl.BlockSpec((N_EXPERTS, 1), const),
            pl.BlockSpec((tm, tm), const),
        ],
        out_specs=[
            pl.BlockSpec((tm, d), row),
            pl.BlockSpec((tm, d), row),
            pl.BlockSpec((8, tm), colblk),
            pl.BlockSpec((tm, LANES), row),
            pl.BlockSpec((8, tm), colblk),
            pl.BlockSpec((N_EXPERTS, LANES), const),
        ],
        out_shape=[
            jax.ShapeDtypeStruct((n, d), F32),
            jax.ShapeDtypeStruct((n, d), F32),
            jax.ShapeDtypeStruct((8, n), jnp.int32),
            jax.ShapeDtypeStruct((n, LANES), F32),
            jax.ShapeDtypeStruct((8, n), jnp.int32),
            jax.ShapeDtypeStruct((N_EXPERTS, LANES), jnp.int32),
        ],
        scratch_shapes=[pltpu.VMEM((N_EXPERTS, LANES), F32)],
        compiler_params=_params("arbitrary"),
        name="out_proj_router",
    )(yr, ys, x2, mod, g, wr, ws, rwh, rwl, rb, before)


def _moe_kernel(te_ref, nu_ref, x_ref, w1_ref, b1_ref, w2_ref, b2_ref, o_ref, w1b_ref, w2b_ref):
    i = pl.program_id(0)
    f = w2_ref.shape[2]
    live = i < nu_ref[0]
    new_expert = jnp.logical_or(i == 0, te_ref[i] != te_ref[jnp.maximum(i - 1, 0)])

    @pl.when(jnp.logical_and(live, new_expert))
    def _():
        w1b_ref[...] = w1_ref[0, 0].astype(BF)
        w2b_ref[...] = w2_ref[0, 0].astype(BF)

    @pl.when(live)
    def _():
        hid = _dot(x_ref[...].astype(BF), w1b_ref[...]) + b1_ref[0, 0]
        glu = jnp.minimum(hid[:, :f], SWIGLU_LIMIT)
        lin = jnp.clip(hid[:, f:], -SWIGLU_LIMIT, SWIGLU_LIMIT)
        act = glu * jax.nn.sigmoid(SWIGLU_ALPHA * glu) * (lin + 1.0)
        o_ref[...] = _dot(act.astype(BF), w2b_ref[...]) + b2_ref[0, 0]

    @pl.when(jnp.logical_not(live))
    def _():
        o_ref[...] = jnp.zeros_like(o_ref)


def _moe_experts(layer, tile_expert, n_used, xs, w1, b1, w2, b2):
    n_rows, half = xs.shape
    _, _, d, f2 = w1.shape
    f = w2.shape[2]
    tm = MOE_TILE
    expert = lambda i, te, nu: (layer, te[i], 0, 0)
    return pl.pallas_call(
        _moe_kernel,
        grid_spec=pltpu.PrefetchScalarGridSpec(
            num_scalar_prefetch=2,
            grid=(n_rows // tm,),
            in_specs=[
                pl.BlockSpec((tm, half), lambda i, te, nu: (i, 0)),
                pl.BlockSpec((1, 1, d, f2), expert),
                pl.BlockSpec((1, 1, 1, f2), expert),
                pl.BlockSpec((1, 1, f, d), expert),
                pl.BlockSpec((1, 1, 1, d), expert),
            ],
            out_specs=pl.BlockSpec((tm, half), lambda i, te, nu: (i, 0)),
            scratch_shapes=[pltpu.VMEM((d, f2), BF), pltpu.VMEM((f, d), BF)],
        ),
        out_shape=jax.ShapeDtypeStruct((n_rows, half), F32),
        compiler_params=pltpu.CompilerParams(dimension_semantics=("arbitrary",),
                                             vmem_limit_bytes=MOE_VMEM_LIMIT),
        name="moe_experts",
    )(tile_expert, n_used, xs, w1, b1.reshape(b1.shape[:2] + (1, f2)), w2,
      b2.reshape(b2.shape[:2] + (1, d)))


def _combine_kernel(x_ref, mod_ref, gate_ref, y0_ref, y1_ref, y2_ref, y3_ref, o_ref):
    moe = None
    for s, y_ref in enumerate((y0_ref, y1_ref, y2_ref, y3_ref)):
        term = gate_ref[:, s:s + 1] * y_ref[...]
        moe = term if moe is None else moe + term
    o_ref[...] = x_ref[...] + mod_ref[0, 5:6, :] * moe


def _combine(x_mid, mod, gate_rows, ys_by_slot, seq):
    n, d = x_mid.shape
    tm = ROW_TILE
    per_seq = seq // tm
    row = lambda i: (i, 0)
    return pl.pallas_call(
        _combine_kernel,
        grid=(n // tm,),
        in_specs=[pl.BlockSpec((tm, d), row),
                  pl.BlockSpec((1,) + mod.shape[1:], lambda i: (i // per_seq, 0, 0)),
                  pl.BlockSpec((tm, LANES), row)]
                 + [pl.BlockSpec((tm, d), row)] * TOP_K,
        out_specs=pl.BlockSpec((tm, d), row),
        out_shape=jax.ShapeDtypeStruct((n, d), F32),
        compiler_params=_params("arbitrary"),
        name="moe_combine",
    )(x_mid, mod, gate_rows, *ys_by_slot)


def _moe_ffn(layer, h_rows, top_idx, rank, counts, w1, b1, w2, b2):
    n_tok = h_rows.shape[0]
    tm = MOE_TILE
    n_assign = n_tok * TOP_K
    n_rows = n_assign + N_EXPERTS * tm
    n_tiles = n_rows // tm
    flat_e = top_idx.T.reshape(-1)
    order = jnp.argsort(flat_e, stable=True)
    padded = (counts + tm - 1) // tm * tm
    padded_end = jnp.cumsum(padded)
    padded_start = padded_end - padded
    start = jnp.cumsum(counts) - counts
    experts = jnp.arange(N_EXPERTS, dtype=jnp.int32)
    pos = rank + jnp.sum(jnp.where(top_idx[..., None] == experts, padded_start, 0), axis=-1)
    tile_start = jnp.arange(n_tiles, dtype=jnp.int32) * tm
    tile_expert = jnp.minimum(
        jnp.sum((tile_start[:, None] >= padded_end[None, :]).astype(jnp.int32), axis=1),
        N_EXPERTS - 1).astype(jnp.int32)
    n_used = (padded_end[-1:] // tm).astype(jnp.int32)
    within = jnp.arange(tm, dtype=jnp.int32)[None, :]
    local = (tile_start - padded_start[tile_expert])[:, None] + within
    src = jnp.clip(start[tile_expert][:, None] + local, 0, n_assign - 1)
    valid = local < counts[tile_expert][:, None]
    row_tok = jnp.where(valid, order[src] // TOP_K, (tile_start[:, None] + within) % n_tok)
    ys = _moe_experts(layer, tile_expert, n_used, h_rows[row_tok.reshape(-1)], w1, b1, w2, b2)
    return [ys[pos[s]] for s in range(TOP_K)]


def _block_diag_mean():
    i = jnp.arange(LANES)
    return jnp.where((i[:, None] // HEAD_DIM) == (i[None, :] // HEAD_DIM), 1.0 / HEAD_DIM, 0.0).astype(BF)


def _block_diag2(a, b):
    top = jnp.concatenate([a, jnp.zeros((a.shape[0], b.shape[1]), a.dtype)], axis=1)
    bot = jnp.concatenate([jnp.zeros((b.shape[0], a.shape[1]), b.dtype), b], axis=1)
    return jnp.concatenate([top, bot], axis=0)


@jax.jit
def _forward(x, c, norm1_g, norm2_g, ada_w, ada_b, w_in, shift_mu, decay_w0, decay_up, iclr_a0,
             iclr_up, gate_up, k_k, k_a, r_k, lnx_w, lnx_b, vres_down, vres_up, vres_b, q_norm_g,
             k_norm_g, sb_out_g, w_out, router_w, router_b, exp_w1, exp_b1, exp_w2, exp_b2):
    bsz, seq, d = x.shape
    depth = w_in.shape[0]
    d_r = decay_w0.shape[1]
    c_rwkv = shift_mu.shape[1]
    d_sb = sb_out_g.shape[1]
    n_tok = bsz * seq
    assert seq % ROW_TILE == 0 and seq % SB_TILE == 0 and seq % RWKV_CHUNK == 0
    assert bsz % RWKV_SEQS == 0 and DECAY_LORA + ICLR_LORA == LANES and GATE_LORA == LANES

    bdm = _block_diag_mean()
    ci = jnp.arange(RWKV_CHUNK)
    tri_chunk = (ci[None, :] <= ci[:, None]).astype(BF)
    si = jnp.arange(SB_TILE)
    tri_sb = jnp.concatenate([(si[:, None] > si[None, :]).astype(BF),
                              jnp.ones((SB_TILE, LANES), BF)], axis=1)
    row_vec = lambda t: t.reshape(1, -1)

    mod_all = _ada_mod(c, ada_w, ada_b)
    x2 = x.reshape(n_tok, d)
    v_first = None
    for l in range(depth):
        mod = mod_all[l]
        zr, q, k, v = _in_proj(
            x2, mod, row_vec(norm1_g[l]), w_in[l].astype(BF), bdm,
            row_vec(jnp.tile(q_norm_g[l], LANES // HEAD_DIM)),
            row_vec(jnp.tile(k_norm_g[l], LANES // HEAD_DIM)), seq, c_rwkv, d_sb)
        if l == 0:
            vres = None
        else:
            rank = vres_down.shape[2]
            vres = (jnp.pad(vres_down[l - 1], ((0, 0), (0, LANES - rank))).astype(BF),
                    jnp.pad(vres_up[l - 1], ((0, LANES - rank), (0, 0))).astype(BF),
                    row_vec(vres_b[l - 1]))
        y_r, v_first = _rwkv_mix(
            zr.reshape(bsz, seq, c_rwkv), v_first, row_vec(shift_mu[l]), row_vec(decay_w0[l]),
            row_vec(iclr_a0[l]), _block_diag2(decay_up[l], iclr_up[l]).astype(BF),
            gate_up[l].astype(BF), row_vec(k_k[l]), row_vec(k_a[l]), row_vec(r_k[l]),
            row_vec(lnx_w[l]), row_vec(lnx_b[l]), vres, tri_chunk, bdm)
        y_s = _sb_attention(q.reshape(bsz, seq, d_sb), k.reshape(bsz, seq, d_sb),
                            v.reshape(bsz, seq, d_sb), row_vec(sb_out_g[l]), tri_sb, bdm)
        rw_t = router_w[l].T
        rwh = rw_t.astype(BF)
        rwl = (rw_t - rwh.astype(F32)).astype(BF)
        w_o = w_out[l].astype(BF)
        x_mid, h2, top_idx, gate_rows, rank, seen = _out_proj(
            y_r.reshape(n_tok, d_r), y_s.reshape(n_tok, d_sb), x2, mod, row_vec(norm2_g[l]),
            w_o[:d_r], w_o[d_r:], rwh, rwl, router_b[l].reshape(N_EXPERTS, 1), seq)
        ys_by_slot = _moe_ffn(l, h2, top_idx[:TOP_K], rank[:TOP_K], seen[:, 0], exp_w1, exp_b1,
                              exp_w2, exp_b2)
        x2 = _combine(x_mid, mod, gate_rows, ys_by_slot, seq)
    return x2.reshape(bsz, seq, d)


def kernel(x, c, norm1_g, norm2_g, ada_w, ada_b, w_in, shift_mu, decay_w0, decay_up, iclr_a0, iclr_up,
           gate_up, k_k, k_a, r_k, lnx_w, lnx_b, vres_down, vres_up, vres_b, q_norm_g, k_norm_g,
           sb_out_g, w_out, router_w, router_b, exp_w1, exp_b1, exp_w2, exp_b2):
    return _forward(x, c, norm1_g, norm2_g, ada_w, ada_b, w_in, shift_mu, decay_w0, decay_up, iclr_a0,
                    iclr_up, gate_up, k_k, k_a, r_k, lnx_w, lnx_b, vres_down, vres_up, vres_b,
                    q_norm_g, k_norm_g, sb_out_g, w_out, router_w, router_b, exp_w1, exp_b1,
                    exp_w2, exp_b2)
```
